```python
import jax, jax.numpy as jnp
from jax import lax
import numpy as np

D_MODEL = 1024
BATCH = 16
SEQ = 4096
DEPTH = 1
DEC_BATCH = 128
DEC_SEQ = 8
PAST_LEN = 8192
PAGE_SIZE = 128

D_MIX = D_MODEL
D_CONV = D_MIX // 2
D_ATT = D_MIX - D_CONV
HEAD_DIM = 64
N_HEADS = D_ATT // HEAD_DIM
CONV_WIDTH = 31
BLOCK = 256
TOPK = 3
Q_CHUNK = 128
D_FF = 2816
D_PLE = 256
LN_EPS = 1e-5
ALPHA = (2.0 * DEPTH) ** 0.25
BETA = (8.0 * DEPTH) ** -0.25
D_IN = 2 * D_CONV + 3 * D_ATT

kernel_name = "hymba_conformer_moba_decoder_step"


def layer_norm(x, g, b):
    xf = x.astype(jnp.float32)
    mu = jnp.mean(xf, axis=-1, keepdims=True)
    var = jnp.mean(jnp.square(xf - mu), axis=-1, keepdims=True)
    return ((xf - mu) * lax.rsqrt(var + LN_EPS)).astype(x.dtype) * g + b


def swiglu(x, w_gate, w_up, w_down):
    return (jax.nn.silu(x @ w_gate) * (x @ w_up)) @ w_down


def alibi_slopes():
    return jnp.exp2(-8.0 * (jnp.arange(N_HEADS, dtype=jnp.float32) + 1.0) / N_HEADS)


def depthwise_conv(u_ext, w, b):
    y = lax.conv_general_dilated(u_ext, w[:, None, :].astype(u_ext.dtype), window_strides=(1,), padding='VALID',
                                 dimension_numbers=('NWC', 'WIO', 'NWC'), feature_group_count=D_CONV)
    return y + b


def conv_mixer(a, g, buf, w_dw, b_dw, cn_g, cn_b):
    u = a * jax.nn.sigmoid(g)
    u_ext = jnp.concatenate([buf.astype(u.dtype), u], axis=1)
    y = jax.nn.silu(layer_norm(depthwise_conv(u_ext, w_dw, b_dw), cn_g, cn_b))
    return y, u_ext[:, u_ext.shape[1] - (CONV_WIDTH - 1):]


def key_blocks(k):
    L = k.shape[0]
    nb = -(-L // BLOCK)
    k = jnp.pad(k, ((0, nb * BLOCK - L), (0, 0), (0, 0)))
    return k.reshape(nb, BLOCK, N_HEADS, HEAD_DIM).transpose(2, 0, 1, 3)


def moba_queries(q, pos, kb, vb, kmean, slopes):
    tq = q.shape[0]
    nb = kb.shape[1]
    own = pos // BLOCK
    gate = jnp.einsum('qhd,hnd->qhn', q.astype(jnp.float32), kmean)
    fully_past = jnp.arange(nb)[None, None, :] < own[:, None, None]
    gate = jnp.where(fully_past, gate, -jnp.inf)
    _, sel = lax.top_k(gate, min(TOPK, nb))
    sel_ok = sel < own[:, None, None]
    own_b = jnp.broadcast_to(own[:, None, None], (tq, N_HEADS, 1))
    blocks = jnp.concatenate([sel, own_b], axis=-1)
    ok = jnp.concatenate([sel_ok, jnp.ones((tq, N_HEADS, 1), dtype=bool)], axis=-1)
    h_idx = jnp.arange(N_HEADS)[None, :, None]
    kg = kb[h_idx, blocks]
    vg = vb[h_idx, blocks]
    kpos = blocks[..., None] * BLOCK + jnp.arange(BLOCK)[None, None, None, :]
    dist = pos[:, None, None, None] - kpos
    valid = ok[..., None] & (dist >= 0)
    s = jnp.einsum('qhd,qhnkd->qhnk', q, kg).astype(jnp.float32) * (HEAD_DIM ** -0.5)
    s = s - slopes[None, :, None, None] * dist.astype(jnp.float32)
    s = jnp.where(valid, s, -jnp.inf)
    p = jax.nn.softmax(s.reshape(tq, N_HEADS, -1), axis=-1).reshape(s.shape)
    return jnp.einsum('qhnk,qhnkd->qhd', p.astype(vg.dtype), vg)


def moba_prompt_seq(q, k, v, slopes):
    t = q.shape[0]
    kb, vb = key_blocks(k), key_blocks(v)
    kmean = jnp.mean(kb.astype(jnp.float32), axis=2)
    n_chunks = t // Q_CHUNK
    qc = q.reshape(n_chunks, Q_CHUNK, N_HEADS, HEAD_DIM)
    pc = jnp.arange(t, dtype=jnp.int32).reshape(n_chunks, Q_CHUNK)
    out = lax.map(lambda a: moba_queries(a[0], a[1], kb, vb, kmean, slopes), (qc, pc))
    return out.reshape(t, N_HEADS, HEAD_DIM)


def moba_sample_seq(q, k_new, v_new, pages, cache_k, cache_v, slopes):
    k_past = cache_k[pages].reshape(-1, N_HEADS, HEAD_DIM)
    v_past = cache_v[pages].reshape(-1, N_HEADS, HEAD_DIM)
    past_len = k_past.shape[0]
    k = jnp.concatenate([k_past.astype(k_new.dtype), k_new], axis=0)
    v = jnp.concatenate([v_past.astype(v_new.dtype), v_new], axis=0)
    kb, vb = key_blocks(k), key_blocks(v)
    kmean = jnp.mean(kb.astype(jnp.float32), axis=2)
    pos = past_len + jnp.arange(q.shape[0], dtype=jnp.int32)
    return moba_queries(q, pos, kb, vb, kmean, slopes)


def token_mixer(h, conv_buf, attn_fn, w_in, b_in, conv_w, conv_b, cn_g, cn_b, w_out):
    B, T = h.shape[0], h.shape[1]
    z = h @ w_in + b_in
    a, g, q, k, v = jnp.split(z, [D_CONV, 2 * D_CONV, 2 * D_CONV + D_ATT, 2 * D_CONV + 2 * D_ATT], axis=-1)
    conv_out, new_buf = conv_mixer(a, g, conv_buf, conv_w, conv_b, cn_g, cn_b)
    q = q.reshape(B, T, N_HEADS, HEAD_DIM)
    k = k.reshape(B, T, N_HEADS, HEAD_DIM)
    v = v.reshape(B, T, N_HEADS, HEAD_DIM)
    att = attn_fn(q, k, v).reshape(B, T, D_ATT)
    y = jnp.concatenate([conv_out, att], axis=-1) @ w_out
    return y, new_buf, k, v


def layer(x, p, conv_buf, attn_fn, w):
    (f1g, f1u, f1d, ln1g, ln1b, w_in, b_in, cw, cb, cng, cnb, w_out, ln2g, ln2b,
     f2g, f2u, f2d, ln3g, ln3b, ple_p, ple_g) = w
    x = layer_norm(ALPHA * x + 0.5 * swiglu(x, f1g, f1u, f1d), ln1g, ln1b)
    m, new_buf, k, v = token_mixer(x, conv_buf, attn_fn, w_in, b_in, cw, cb, cng, cnb, w_out)
    x = layer_norm(ALPHA * x + m, ln2g, ln2b)
    x = layer_norm(ALPHA * x + 0.5 * swiglu(x, f2g, f2u, f2d), ln3g, ln3b)
    x = x + jax.nn.sigmoid(x @ ple_g) * (p @ ple_p)
    return x, new_buf, k, v


def setup_inputs(seed: int = 0) -> dict:
    key = jax.random.key(seed)
    ks = iter(jax.random.split(key, 40))

    def nrm(shape, scale):
        return jax.random.normal(next(ks), shape, jnp.float32) * scale

    n_pages = PAST_LEN // PAGE_SIZE
    n_used = DEC_BATCH * n_pages
    n_pool = n_used + max(1, n_used // 4)
    page_table = jax.random.permutation(next(ks), n_pool)[:n_used].reshape(DEC_BATCH, n_pages).astype(jnp.int32)
    return {
        "x_prompt": nrm((BATCH, SEQ, D_MODEL), 1.0),
        "x_sample": nrm((DEC_BATCH, DEC_SEQ, D_MODEL), 1.0),
        "cache_k": nrm((DEPTH, n_pool, PAGE_SIZE, N_HEADS, HEAD_DIM), 1.0),
        "cache_v": nrm((DEPTH, n_pool, PAGE_SIZE, N_HEADS, HEAD_DIM), 1.0),
        "state_conv": nrm((DEPTH, DEC_BATCH, CONV_WIDTH - 1, D_CONV), 0.5),
        "page_table": page_table,
        "p_prompt": nrm((DEPTH, BATCH, SEQ, D_PLE), 1.0),
        "p_sample": nrm((DEPTH, DEC_BATCH, DEC_SEQ, D_PLE), 1.0),
        "ffn1_w_gate": nrm((DEPTH, D_MODEL, D_FF), D_MODEL ** -0.5),
        "ffn1_w_up": nrm((DEPTH, D_MODEL, D_FF), D_MODEL ** -0.5),
        "ffn1_w_down": nrm((DEPTH, D_FF, D_MODEL), BETA * D_FF ** -0.5),
        "ln1_g": 1.0 + nrm((DEPTH, D_MODEL), 0.02),
        "ln1_b": nrm((DEPTH, D_MODEL), 0.02),
        "w_in": nrm((DEPTH, D_MODEL, D_IN), D_MODEL ** -0.5),
        "b_in": nrm((DEPTH, D_IN), 0.02),
        "conv_w": nrm((DEPTH, CONV_WIDTH, D_CONV), CONV_WIDTH ** -0.5),
        "conv_b": nrm((DEPTH, D_CONV), 0.02),
        "conv_norm_g": 1.0 + nrm((DEPTH, D_CONV), 0.02),
        "conv_norm_b": nrm((DEPTH, D_CONV), 0.02),
        "w_out": nrm((DEPTH, D_MIX, D_MODEL), BETA * D_MIX ** -0.5),
        "ln2_g": 1.0 + nrm((DEPTH, D_MODEL), 0.02),
        "ln2_b": nrm((DEPTH, D_MODEL), 0.02),
        "ffn2_w_gate": nrm((DEPTH, D_MODEL, D_FF), D_MODEL ** -0.5),
        "ffn2_w_up": nrm((DEPTH, D_MODEL, D_FF), D_MODEL ** -0.5),
        "ffn2_w_down": nrm((DEPTH, D_FF, D_MODEL), BETA * D_FF ** -0.5),
        "ln3_g": 1.0 + nrm((DEPTH, D_MODEL), 0.02),
        "ln3_b": nrm((DEPTH, D_MODEL), 0.02),
        "ple_w_proj": nrm((DEPTH, D_PLE, D_MODEL), D_PLE ** -0.5),
        "ple_w_gate": nrm((DEPTH, D_MODEL, D_MODEL), D_MODEL ** -0.5),
    }


def reference(x_prompt, x_sample, cache_k, cache_v, state_conv, page_table, p_prompt, p_sample,
              ffn1_w_gate, ffn1_w_up, ffn1_w_down, ln1_g, ln1_b, w_in, b_in, conv_w, conv_b,
              conv_norm_g, conv_norm_b, w_out, ln2_g, ln2_b, ffn2_w_gate, ffn2_w_up, ffn2_w_down,
              ln3_g, ln3_b, ple_w_proj, ple_w_gate):
    slopes = alibi_slopes()
    yp, ys = x_prompt, x_sample
    kp_l, vp_l, cp_l, ks_l, vs_l, cs_l = [], [], [], [], [], []

    def attn_prompt(q, k, v):
        return lax.map(lambda a: moba_prompt_seq(a[0], a[1], a[2], slopes), (q, k, v))

    for i in range(DEPTH):
        w = (ffn1_w_gate[i], ffn1_w_up[i], ffn1_w_down[i], ln1_g[i], ln1_b[i], w_in[i], b_in[i],
             conv_w[i], conv_b[i], conv_norm_g[i], conv_norm_b[i], w_out[i], ln2_g[i], ln2_b[i],
             ffn2_w_gate[i], ffn2_w_up[i], ffn2_w_down[i], ln3_g[i], ln3_b[i], ple_w_proj[i], ple_w_gate[i])
        ck, cv = cache_k[i], cache_v[i]

        def attn_sample(q, k, v, ck=ck, cv=cv):
            return lax.map(lambda a: moba_sample_seq(a[0], a[1], a[2], a[3], ck, cv, slopes),
                           (q, k, v, page_table))

        zero_buf = jnp.zeros((yp.shape[0], CONV_WIDTH - 1, D_CONV), yp.dtype)
        yp, bp, kp, vp = layer(yp, p_prompt[i], zero_buf, attn_prompt, w)
        ys, bs, ksm, vsm = layer(ys, p_sample[i], state_conv[i], attn_sample, w)
        kp_l.append(kp); vp_l.append(vp); cp_l.append(bp)
        ks_l.append(ksm); vs_l.append(vsm); cs_l.append(bs)
    return (yp, ys, jnp.stack(kp_l), jnp.stack(vp_l), jnp.stack(cp_l),
            jnp.stack(ks_l), jnp.stack(vs_l), jnp.stack(cs_l))
```

```python
import functools

import jax
import jax.numpy as jnp
from jax import lax
from jax.experimental import pallas as pl
from jax.experimental.pallas import tpu as pltpu

F32 = jnp.float32
BF16 = jnp.bfloat16

LN_EPS = 1e-5
HEAD_DIM = 64
BLOCK = 256
TOPK = 3
CONV_WIDTH = 31
CONV_HALO = 32
MASKED = -1e30
SEL_ROWS = 16
AUG = 128
V_ROWS = 80
LANES = 128
VMEM_LIMIT_BYTES = 56 * 1024 * 1024


def _dot(a, b):
    return jnp.dot(a, b, preferred_element_type=F32)


def _dot_nt(a, b):
    return lax.dot_general(a, b, (((1,), (1,)), ((), ())), preferred_element_type=F32)


def _sigmoid(x):
    return 1.0 / (1.0 + jnp.exp(-x))


def _layer_norm(r, g, b):
    mu = jnp.mean(r, axis=-1, keepdims=True)
    d = r - mu
    var = jnp.mean(d * d, axis=-1, keepdims=True)
    return d * lax.rsqrt(var + LN_EPS) * g + b


def _swiglu(xb, wg_ref, wu_ref, wd_ref):
    g = _dot(xb, wg_ref[...])
    u = _dot(xb, wu_ref[...])
    h = (g * _sigmoid(g)) * u
    return _dot(h.astype(BF16), wd_ref[...])


def _const_spec(shape):
    nd = len(shape)
    return pl.BlockSpec(shape, lambda *_: (0,) * nd, pipeline_mode=pl.Buffered(1))


def _params(n_axes):
    return pltpu.CompilerParams(dimension_semantics=("arbitrary",) * n_axes,
                                vmem_limit_bytes=VMEM_LIMIT_BYTES)


def _row_tile(n):
    for tm in (512, 256, 128, 64, 32, 16, 8):
        if n % tm == 0:
            return tm
    raise ValueError(f"row count {n} is not a multiple of 8")


def _ffn_ln_kernel(x_ref, wg_ref, wu_ref, wd_ref, g_ref, b_ref, o_ref, *, alpha):
    x = x_ref[...]
    y = _swiglu(x.astype(BF16), wg_ref, wu_ref, wd_ref)
    o_ref[...] = _layer_norm(alpha * x + 0.5 * y, g_ref[...], b_ref[...])


def _ffn_ln(x, wg, wu, wd, g, b, alpha):
    n, d = x.shape
    tm = _row_tile(n)
    return pl.pallas_call(
        functools.partial(_ffn_ln_kernel, alpha=alpha),
        grid=(n // tm,),
        in_specs=[pl.BlockSpec((tm, d), lambda i: (i, 0)),
                  _const_spec(wg.shape), _const_spec(wu.shape), _const_spec(wd.shape),
                  _const_spec(g.shape), _const_spec(b.shape)],
        out_specs=pl.BlockSpec((tm, d), lambda i: (i, 0)),
        out_shape=jax.ShapeDtypeStruct((n, d), F32),
        compiler_params=_params(1),
        name="ffn_ln",
    )(x, wg, wu, wd, g, b)


def _out_ffn_ple_kernel(x1_ref, conv_ref, att_ref, p_ref, wo_ref, g2_ref, b2_ref,
                        wg_ref, wu_ref, wd_ref, g3_ref, b3_ref, pg_ref, pp_ref, o_ref,
                        *, alpha, d_conv):
    m = _dot(conv_ref[...], wo_ref[:d_conv, :]) + _dot(att_ref[...], wo_ref[d_conv:, :])
    x2 = _layer_norm(alpha * x1_ref[...] + m, g2_ref[...], b2_ref[...])
    y = _swiglu(x2.astype(BF16), wg_ref, wu_ref, wd_ref)
    x3 = _layer_norm(alpha * x2 + 0.5 * y, g3_ref[...], b3_ref[...])
    gate = _sigmoid(_dot(x3.astype(BF16), pg_ref[...]))
    o_ref[...] = x3 + gate * _dot(p_ref[...].astype(BF16), pp_ref[...])


def _out_ffn_ple(x1, conv, att, p, wo, g2, b2, wg, wu, wd, g3, b3, pg, pp, alpha):
    n, d = x1.shape
    tm = min(_row_tile(n), 256)
    rows = lambda a: pl.BlockSpec((tm, a.shape[1]), lambda i: (i, 0))
    consts = (wo, g2, b2, wg, wu, wd, g3, b3, pg, pp)
    return pl.pallas_call(
        functools.partial(_out_ffn_ple_kernel, alpha=alpha, d_conv=conv.shape[1]),
        grid=(n // tm,),
        in_specs=[rows(x1), rows(conv), rows(att), rows(p)] + [_const_spec(c.shape) for c in consts],
        out_specs=rows(x1),
        out_shape=jax.ShapeDtypeStruct((n, d), F32),
        compiler_params=_params(1),
        name="out_ffn_ple",
    )(x1, conv, att, p, *consts)


def _conv_branch_tail(y, cb_ref, cng_ref, cnb_ref):
    c = _layer_norm(y + cb_ref[...], cng_ref[...], cnb_ref[...])
    return c * _sigmoid(c)


def _inproj_prompt_kernel(x1_ref, w_in_ref, b_in_ref, cw_ref, cb_ref, cng_ref, cnb_ref,
                          k_ref, v_ref, conv_ref, utail_ref, kmean_ref, qT_ref, kaug_ref, vT_ref,
                          uext_ref, *, tm, n_heads, d_conv, d_att):
    t = pl.program_id(1)
    bpt = tm // BLOCK
    z = _dot(x1_ref[0].astype(BF16), w_in_ref[...]) + b_in_ref[...]
    a = z[:, :d_conv]
    g = z[:, d_conv:2 * d_conv]
    zq = z[:, 2 * d_conv:2 * d_conv + d_att]
    zk = z[:, 2 * d_conv + d_att:2 * d_conv + 2 * d_att]
    zv = z[:, 2 * d_conv + 2 * d_att:]
    k_ref[0] = zk
    v_ref[0] = zv

    @pl.when(t == 0)
    def _():
        uext_ref[0:CONV_HALO, :] = jnp.zeros((CONV_HALO, d_conv), F32)

    uext_ref[CONV_HALO:CONV_HALO + tm, :] = a * _sigmoid(g)
    off = CONV_HALO - (CONV_WIDTH - 1)
    y = jnp.zeros((tm, d_conv), F32)
    for j in range(CONV_WIDTH):
        y = y + cw_ref[j:j + 1, :] * uext_ref[off + j:off + j + tm, :]
    conv_ref[0] = _conv_branch_tail(y, cb_ref, cng_ref, cnb_ref).astype(BF16)
    tail = uext_ref[tm:tm + CONV_HALO, :]
    utail_ref[0] = tail
    uext_ref[0:CONV_HALO, :] = tail

    @pl.when(t == 0)
    def _():
        kmean_ref[0] = jnp.zeros(kmean_ref.shape[1:], F32)

    km = kmean_ref[0]
    km_row = lax.broadcasted_iota(jnp.int32, km.shape, 0)
    for blk in range(bpt):
        mean = jnp.sum(zk[blk * BLOCK:(blk + 1) * BLOCK], axis=0, keepdims=True) * (1.0 / BLOCK)
        km = jnp.where(km_row == t * bpt + blk, mean, km)
    kmean_ref[0] = km

    zqT = (zq * (HEAD_DIM ** -0.5)).T
    zvT = zv.T
    lane = lax.broadcasted_iota(jnp.int32, (BLOCK, LANES), 1)
    pos = lax.broadcasted_iota(jnp.int32, (BLOCK, LANES), 0).astype(F32)
    col = lane - HEAD_DIM
    ones_row = jnp.where(lax.broadcasted_iota(jnp.int32, (V_ROWS - HEAD_DIM, BLOCK), 0) == 0, 1.0, 0.0)
    for blk in range(bpt):
        j = t * bpt + blk
        rows = slice(blk * BLOCK, (blk + 1) * BLOCK)
        for h in range(n_heads):
            slope = 2.0 ** (-8.0 * (h + 1) / n_heads)
            hd = slice(h * HEAD_DIM, (h + 1) * HEAD_DIM)
            qT_ref[0, h, blk] = zqT[hd, rows].astype(BF16)
            vT_ref[0, h, blk] = jnp.concatenate([zvT[hd, rows], ones_row], axis=0).astype(BF16)
            pair = zk[rows, (h // 2) * LANES:(h // 2 + 1) * LANES]
            if h % 2:
                pair = pltpu.roll(pair, HEAD_DIM, 1)
            extra = (jnp.where(col == j, 1.0, 0.0)
                     + jnp.where(col == SEL_ROWS, slope * pos, 0.0)
                     + jnp.where(col == SEL_ROWS + 1, (slope * BLOCK) * j.astype(F32), 0.0))
            kaug_ref[0, h, blk] = jnp.where(lane < HEAD_DIM, pair, extra).astype(BF16)


def _inproj_prompt(x1, w_in, b_in, cw, cb, cng, cnb, n_heads):
    bsz, t_len, d = x1.shape
    d_conv = cw.shape[1]
    d_att = n_heads * HEAD_DIM
    tm = 512 if t_len % 512 == 0 else BLOCK
    bpt = tm // BLOCK
    nb = t_len // BLOCK
    tiles = lambda w: pl.BlockSpec((1, tm, w), lambda b, t: (b, t, 0))
    per_seq = lambda r, w: pl.BlockSpec((1, r, w), lambda b, t: (b, 0, 0))
    head_blocks = lambda r, w: pl.BlockSpec((1, n_heads, bpt, r, w), lambda b, t: (b, 0, t, 0, 0))
    consts = (w_in, b_in, cw, cb, cng, cnb)
    return pl.pallas_call(
        functools.partial(_inproj_prompt_kernel, tm=tm, n_heads=n_heads, d_conv=d_conv, d_att=d_att),
        grid=(bsz, t_len // tm),
        in_specs=[tiles(d)] + [_const_spec(c.shape) for c in consts],
        out_specs=[tiles(d_att), tiles(d_att), tiles(d_conv), per_seq(CONV_HALO, d_conv), per_seq(nb, d_att),
                   head_blocks(HEAD_DIM, BLOCK), head_blocks(BLOCK, AUG), head_blocks(V_ROWS, BLOCK)],
        out_shape=[jax.ShapeDtypeStruct((bsz, t_len, d_att), F32),
                   jax.ShapeDtypeStruct((bsz, t_len, d_att), F32),
                   jax.ShapeDtypeStruct((bsz, t_len, d_conv), BF16),
                   jax.ShapeDtypeStruct((bsz, CONV_HALO, d_conv), F32),
                   jax.ShapeDtypeStruct((bsz, nb, d_att), F32),
                   jax.ShapeDtypeStruct((bsz, n_heads, nb, HEAD_DIM, BLOCK), BF16),
                   jax.ShapeDtypeStruct((bsz, n_heads, nb, BLOCK, AUG), BF16),
                   jax.ShapeDtypeStruct((bsz, n_heads, nb, V_ROWS, BLOCK), BF16)],
        scratch_shapes=[pltpu.VMEM((CONV_HALO + tm, d_conv), F32)],
        compiler_params=_params(2),
        name="inproj_prompt",
    )(x1, *consts)


def _topk_select(gate, idx, n_valid):
    n = gate.shape[0]
    valid = idx < n_valid
    gm = jnp.where(valid, gate, -jnp.inf)
    rank = jnp.zeros(gate.shape, jnp.int32)
    for m in range(n):
        row = gm[m:m + 1, :]
        rank = rank + jnp.where(row > gm, 1, jnp.where(row == gm, jnp.where(idx > m, 1, 0), 0))
    return (rank < TOPK) & valid


def _moba_prompt_kernel(qT_ref, kaug_ref, vT_ref, kmean_ref, o_ref, *, nb, heads):
    blk_idx = lax.broadcasted_iota(jnp.int32, (nb, BLOCK), 0)
    causal = (lax.broadcasted_iota(jnp.int32, (BLOCK, BLOCK), 0)
              <= lax.broadcasted_iota(jnp.int32, (BLOCK, BLOCK), 1))
    aug_row = lax.broadcasted_iota(jnp.int32, (AUG - HEAD_DIM - SEL_ROWS, BLOCK), 0)
    bias_rows = jnp.where(aug_row < 2, 1.0, 0.0).astype(BF16)

    def q_tile(i, carry):
        outs = []
        for hh in range(heads):
            qT = qT_ref[0, hh, i]
            gate = _dot(kmean_ref[0, hh].astype(BF16), qT)
            keep = _topk_select(gate, blk_idx, i) | (blk_idx == i)
            selbias = jnp.where(keep, 0.0, MASKED)
            if nb < SEL_ROWS:
                selbias = jnp.concatenate([selbias, jnp.zeros((SEL_ROWS - nb, BLOCK), F32)], axis=0)
            qaug = jnp.concatenate([qT, selbias.astype(BF16), bias_rows], axis=0)

            st = jnp.where(causal, _dot(kaug_ref[0, hh, i], qaug), MASKED)
            m0 = jnp.max(st, axis=0, keepdims=True)
            acc0 = _dot(vT_ref[0, hh, i], jnp.exp(st - m0).astype(BF16))

            def past_block(j, c, hh=hh, qaug=qaug):
                m, acc = c
                st = _dot(kaug_ref[0, hh, j], qaug)
                m_new = jnp.maximum(m, jnp.max(st, axis=0, keepdims=True))
                p = jnp.exp(st - m_new).astype(BF16)
                return m_new, acc * jnp.exp(m - m_new) + _dot(vT_ref[0, hh, j], p)

            _, acc = lax.fori_loop(0, i, past_block, (m0, acc0))
            outs.append(acc[:HEAD_DIM] / acc[HEAD_DIM:HEAD_DIM + 1])
        q0 = pl.multiple_of(i * BLOCK, BLOCK)
        o_ref[0, pl.ds(q0, BLOCK), :] = jnp.concatenate(outs, axis=0).T.astype(BF16)
        return carry

    lax.fori_loop(0, nb, q_tile, 0)


def _moba_prompt(qT, kaug, vT, kmean):
    bsz, n_heads, nb = qT.shape[:3]
    assert nb <= SEL_ROWS, "prompt sequences longer than SEL_ROWS key blocks are not supported"
    heads = 2
    spec = lambda r, w: pl.BlockSpec((1, heads, nb, r, w), lambda b, g: (b, g, 0, 0, 0))
    return pl.pallas_call(
        functools.partial(_moba_prompt_kernel, nb=nb, heads=heads),
        grid=(bsz, n_heads // heads),
        in_specs=[spec(HEAD_DIM, BLOCK), spec(BLOCK, AUG), spec(V_ROWS, BLOCK),
                  pl.BlockSpec((1, heads, nb, HEAD_DIM), lambda b, g: (b, g, 0, 0))],
        out_specs=pl.BlockSpec((1, nb * BLOCK, heads * HEAD_DIM), lambda b, g: (b, 0, g)),
        out_shape=jax.ShapeDtypeStruct((bsz, nb * BLOCK, n_heads * HEAD_DIM), BF16),
        compiler_params=_params(2),
        name="moba_prompt",
    )(qT, kaug, vT, kmean)


def _inproj_sample_kernel(x1_ref, w_in_ref, b_in_ref, cw_ref, cb_ref, cng_ref, cnb_ref, state_ref,
                          q_ref, k_ref, v_ref, u_ref, conv_ref, *, bs, ts, d_conv, d_att):
    z = _dot(x1_ref[...].astype(BF16), w_in_ref[...]) + b_in_ref[...]
    a = z[:, :d_conv]
    g = z[:, d_conv:2 * d_conv]
    q_ref[...] = z[:, 2 * d_conv:2 * d_conv + d_att]
    k_ref[...] = z[:, 2 * d_conv + d_att:2 * d_conv + 2 * d_att]
    v_ref[...] = z[:, 2 * d_conv + 2 * d_att:]
    u_ref[...] = a * _sigmoid(g)
    n_state = CONV_WIDTH - 1
    for t in range(ts):
        y = jnp.zeros((bs, d_conv), F32)
        for j in range(CONV_WIDTH):
            tau = t + j
            if tau < n_state:
                slab = state_ref[tau]
            else:
                slab = u_ref[(tau - n_state) * bs:(tau - n_state + 1) * bs, :]
            y = y + cw_ref[j:j + 1, :] * slab
        conv_ref[t * bs:(t + 1) * bs, :] = _conv_branch_tail(y, cb_ref, cng_ref, cnb_ref).astype(BF16)


def _inproj_sample(x1, w_in, b_in, cw, cb, cng, cnb, state_tm, n_heads):
    n, d = x1.shape
    bs = state_tm.shape[1]
    ts = n // bs
    d_conv = cw.shape[1]
    d_att = n_heads * HEAD_DIM
    ins = (x1, w_in, b_in, cw, cb, cng, cnb, state_tm)
    out = lambda w, dt: jax.ShapeDtypeStruct((n, w), dt)
    outs = [out(d_att, F32), out(d_att, F32), out(d_att, F32), out(d_conv, F32), out(d_conv, BF16)]
    return pl.pallas_call(
        functools.partial(_inproj_sample_kernel, bs=bs, ts=ts, d_conv=d_conv, d_att=d_att),
        grid=(1,),
        in_specs=[_const_spec(a.shape) for a in ins],
        out_specs=[pl.BlockSpec(o.shape, lambda i: (0, 0)) for o in outs],
        out_shape=outs,
        compiler_params=_params(1),
        name="inproj_sample",
    )(*ins)


def _head_slopes(shape, n_heads, rows_per_head):
    head = lax.div(lax.broadcasted_iota(jnp.int32, shape, 0), jnp.int32(rows_per_head))
    out = jnp.zeros(shape, F32)
    for h in range(n_heads):
        out = jnp.where(head == h, 2.0 ** (-8.0 * (h + 1) / n_heads), out)
    return out


def _moba_sample_kernel(pt_ref, q_ref, kn_ref, vn_ref, *refs, pages, n_heads, ts, nbs):
    del pt_ref
    kp = refs[:pages]
    vp = refs[pages:2 * pages]
    o_ref = refs[2 * pages]
    kmean_scr, m_scr, l_scr, o_scr = refs[2 * pages + 1:]
    c = pl.program_id(1)
    d_att = n_heads * HEAD_DIM
    rows = n_heads * ts
    page = kp[0].shape[1]
    bpc = pages * page // BLOCK
    ppb = BLOCK // page

    qs = q_ref[0] * (HEAD_DIM ** -0.5)
    qrep = jnp.concatenate([qs] * n_heads, axis=0)
    row_head = lax.div(lax.broadcasted_iota(jnp.int32, (rows, d_att), 0), jnp.int32(ts))
    lane_head = lax.div(lax.broadcasted_iota(jnp.int32, (rows, d_att), 1), jnp.int32(HEAD_DIM))
    qbd = jnp.where(row_head == lane_head, qrep, 0.0).astype(BF16)
    slope = _head_slopes((rows, 1), n_heads, ts)
    key_bias = slope * lax.broadcasted_iota(jnp.int32, (rows, BLOCK), 1).astype(F32)

    @pl.when(c == 0)
    def _():
        kmean_scr[...] = jnp.zeros(kmean_scr.shape, F32)
        m_scr[...] = jnp.full(m_scr.shape, MASKED, F32)
        l_scr[...] = jnp.zeros(l_scr.shape, F32)

    km_row = lax.broadcasted_iota(jnp.int32, kmean_scr.shape, 0)
    stat_lane = lax.broadcasted_iota(jnp.int32, m_scr.shape, 1)
    for blk in range(bpc):
        n = c * bpc + blk
        ks = [kp[blk * ppb + i][0] for i in range(ppb)]
        vs = [vp[blk * ppb + i][0] for i in range(ppb)]
        ksum = ks[0].sum(axis=0, keepdims=True)
        for kpage in ks[1:]:
            ksum = ksum + kpage.sum(axis=0, keepdims=True)
        kmean_scr[...] = jnp.where(km_row == n, ksum * (1.0 / BLOCK), kmean_scr[...])
        s = jnp.concatenate([_dot_nt(qbd, kpage.astype(BF16)) for kpage in ks], axis=1) + key_bias
        mb = jnp.max(s, axis=1, keepdims=True)
        p = jnp.exp(s - mb)
        lb = jnp.sum(p, axis=1, keepdims=True)
        pb = p.astype(BF16)
        o = _dot(pb[:, :page], vs[0].astype(BF16))
        for i in range(1, ppb):
            o = o + _dot(pb[:, i * page:(i + 1) * page], vs[i].astype(BF16))
        for h in range(n_heads):
            lt = (h * HEAD_DIM) // LANES
            o_scr[n, h * ts:(h + 1) * ts, :] = o[h * ts:(h + 1) * ts, lt * LANES:(lt + 1) * LANES]
        m_scr[...] = jnp.where(stat_lane == n, mb, m_scr[...])
        l_scr[...] = jnp.where(stat_lane == n, lb, l_scr[...])

    @pl.when(c == pl.num_programs(1) - 1)
    def _():
        blk_lane = lax.broadcasted_iota(jnp.int32, (rows, nbs), 1)
        gate = _dot_nt(qbd, kmean_scr[...].astype(BF16))
        rank = jnp.zeros((rows, nbs), jnp.int32)
        for m in range(nbs):
            colm = gate[:, m:m + 1]
            rank = rank + jnp.where(colm > gate, 1, jnp.where(colm == gate, jnp.where(blk_lane > m, 1, 0), 0))
        sel = rank < TOPK
        blk_bias = slope * ((blk_lane - nbs) * BLOCK).astype(F32)
        e = jnp.where(sel, m_scr[:, :nbs] + blk_bias, MASKED)

        pad = jnp.zeros((16 - ts, d_att), F32)
        kn = jnp.concatenate([kn_ref[0], pad], axis=0).astype(BF16)
        vn = jnp.concatenate([vn_ref[0], pad], axis=0).astype(BF16)
        u_idx = lax.broadcasted_iota(jnp.int32, (rows, 16), 1)
        q_idx = lax.rem(lax.broadcasted_iota(jnp.int32, (rows, 16), 0), jnp.int32(ts))
        so = _dot_nt(qbd, kn) + slope * u_idx.astype(F32)
        so = jnp.where(u_idx <= q_idx, so, MASKED)
        mo = jnp.max(so, axis=1, keepdims=True)
        po = jnp.exp(so - mo)
        lo = jnp.sum(po, axis=1, keepdims=True)
        oo = _dot(po.astype(BF16), vn)
        own = jnp.concatenate(
            [oo[h * ts:(h + 1) * ts, ((h * HEAD_DIM) // LANES) * LANES:((h * HEAD_DIM) // LANES + 1) * LANES]
             for h in range(n_heads)], axis=0)

        mx = jnp.maximum(jnp.max(e, axis=1, keepdims=True), mo)
        w = jnp.where(sel, jnp.exp(e - mx), 0.0)
        wo = jnp.exp(mo - mx)
        denom = jnp.sum(w * l_scr[:, :nbs], axis=1, keepdims=True) + wo * lo
        num = wo * own
        for n in range(nbs):
            num = num + w[:, n:n + 1] * o_scr[n]
        res = num / denom
        pieces = []
        for h in range(n_heads):
            l0 = (h * HEAD_DIM) % LANES
            pieces.append(res[h * ts:(h + 1) * ts, l0:l0 + HEAD_DIM])
        o_ref[0] = jnp.concatenate(pieces, axis=-1)


def _moba_sample(q, k_new, v_new, page_table, cache_k, cache_v, n_heads):
    bs, ts, d_att = q.shape
    n_pages = page_table.shape[1]
    page = cache_k.shape[1]
    assert BLOCK % page == 0 and (n_pages * page) % BLOCK == 0 and ts <= 8
    nbs = n_pages * page // BLOCK
    pages = 8 if n_pages % 8 == 0 else BLOCK // page
    rows = n_heads * ts
    assert nbs <= LANES
    pt = page_table.reshape(-1).astype(jnp.int32)
    seq = pl.BlockSpec((1, ts, d_att), lambda b, c, pt: (b, 0, 0))

    def page_spec(p):
        return pl.BlockSpec((1, page, d_att), lambda b, c, pt: (pt[b * n_pages + c * pages + p], 0, 0))

    grid_spec = pltpu.PrefetchScalarGridSpec(
        num_scalar_prefetch=1,
        grid=(bs, n_pages // pages),
        in_specs=[seq, seq, seq] + [page_spec(p) for p in range(pages)] * 2,
        out_specs=seq,
        scratch_shapes=[pltpu.VMEM((max(nbs, 8), d_att), F32),
                        pltpu.VMEM((rows, LANES), F32),
                        pltpu.VMEM((rows, LANES), F32),
                        pltpu.VMEM((nbs, rows, LANES), F32)],
    )
    return pl.pallas_call(
        functools.partial(_moba_sample_kernel, pages=pages, n_heads=n_heads, ts=ts, nbs=nbs),
        grid_spec=grid_spec,
        out_shape=jax.ShapeDtypeStruct((bs, ts, d_att), F32),
        compiler_params=_params(2),
        name="moba_sample",
    )(pt, q, k_new, v_new, *([cache_k] * pages), *([cache_v] * pages))


def kernel(x_prompt, x_sample, cache_k, cache_v, state_conv, page_table, p_prompt, p_sample,
           ffn1_w_gate, ffn1_w_up, ffn1_w_down, ln1_g, ln1_b, w_in, b_in, conv_w, conv_b,
           conv_norm_g, conv_norm_b, w_out, ln2_g, ln2_b, ffn2_w_gate, ffn2_w_up, ffn2_w_down,
           ln3_g, ln3_b, ple_w_proj, ple_w_gate):
    depth = ffn1_w_gate.shape[0]
    alpha = (2.0 * depth) ** 0.25
    bsz, t_len, d = x_prompt.shape
    bs, ts, _ = x_sample.shape
    n_heads = cache_k.shape[3]
    assert cache_k.shape[4] == HEAD_DIM and conv_w.shape[1] == CONV_WIDTH
    assert n_heads == 8, "ALiBi slopes must be powers of two for the exact bf16 bias columns"
    assert t_len % BLOCK == 0
    d_att = n_heads * HEAD_DIM
    d_conv = conv_w.shape[2]
    n_state = CONV_WIDTH - 1
    row = lambda a: a.reshape(1, -1)

    yp = x_prompt.reshape(bsz * t_len, d)
    ys = x_sample.transpose(1, 0, 2).reshape(ts * bs, d)
    outs = [[] for _ in range(6)]
    for i in range(depth):
        f1 = (ffn1_w_gate[i].astype(BF16), ffn1_w_up[i].astype(BF16), ffn1_w_down[i].astype(BF16),
              row(ln1_g[i]), row(ln1_b[i]))
        mix = (w_in[i].astype(BF16), row(b_in[i]), conv_w[i], row(conv_b[i]),
               row(conv_norm_g[i]), row(conv_norm_b[i]))
        tail = (w_out[i].astype(BF16), row(ln2_g[i]), row(ln2_b[i]),
                ffn2_w_gate[i].astype(BF16), ffn2_w_up[i].astype(BF16), ffn2_w_down[i].astype(BF16),
                row(ln3_g[i]), row(ln3_b[i]), ple_w_gate[i].astype(BF16), ple_w_proj[i].astype(BF16))

        x1 = _ffn_ln(yp, *f1, alpha)
        k, v, conv, utail, kmean, qT, kaug, vT = _inproj_prompt(x1.reshape(bsz, t_len, d), *mix, n_heads)
        nb = t_len // BLOCK
        kmean_h = kmean.reshape(bsz, nb, n_heads, HEAD_DIM).transpose(0, 2, 1, 3)
        att = _moba_prompt(qT, kaug, vT, kmean_h)
        yp = _out_ffn_ple(x1, conv.reshape(-1, d_conv), att.reshape(-1, d_att),
                          p_prompt[i].reshape(bsz * t_len, -1), *tail, alpha)
        outs[0].append(k.reshape(bsz, t_len, n_heads, HEAD_DIM))
        outs[1].append(v.reshape(bsz, t_len, n_heads, HEAD_DIM))
        outs[2].append(utail[:, CONV_HALO - n_state:])

        x1 = _ffn_ln(ys, *f1, alpha)
        state = state_conv[i]
        q, k, v, u, conv = _inproj_sample(x1, *mix, state.transpose(1, 0, 2), n_heads)
        to_bm = lambda a: a.reshape(ts, bs, -1).transpose(1, 0, 2)
        q, k, v, u = to_bm(q), to_bm(k), to_bm(v), to_bm(u)
        att = _moba_sample(q, k, v, page_table,
                           cache_k[i].reshape(cache_k.shape[1], cache_k.shape[2], d_att),
                           cache_v[i].reshape(cache_v.shape[1], cache_v.shape[2], d_att), n_heads)
        att = att.transpose(1, 0, 2).reshape(ts * bs, d_att).astype(BF16)
        ys = _out_ffn_ple(x1, conv, att, p_sample[i].transpose(1, 0, 2).reshape(ts * bs, -1), *tail, alpha)
        outs[3].append(k.reshape(bs, ts, n_heads, HEAD_DIM))
        outs[4].append(v.reshape(bs, ts, n_heads, HEAD_DIM))
        outs[5].append(jnp.concatenate([state, u], axis=1)[:, ts:])

    return (yp.reshape(bsz, t_len, d), ys.reshape(ts, bs, d).transpose(1, 0, 2),
            *(jnp.stack(o) for o in outs))
```

```python
import functools

import jax
import jax.numpy as jnp
from jax import lax
from jax.experimental import pallas as pl
from jax.experimental.pallas import tpu as pltpu

F32 = jnp.float32
BF16 = jnp.bfloat16

LN_EPS = 1e-5
HEAD_DIM = 64
BLOCK = 256
TOPK = 3
CONV_WIDTH = 31
CONV_HALO = 32
CONV_ROWS = 32
SUBLANES = 8
MASKED = -(2.0 ** 100)
SEL_ROWS = 16
AUG = 128
V_ROWS = 80
LANES = 128
VMEM_LIMIT_BYTES = 56 * 1024 * 1024


def _dot(a, b):
    return jnp.dot(a, b, preferred_element_type=F32)


def _dot_nt(a, b):
    return lax.dot_general(a, b, (((1,), (1,)), ((), ())), preferred_element_type=F32)


def _sigmoid(x):
    return 1.0 / (1.0 + jnp.exp(-x))


def _layer_norm(r, g, b):
    mu = jnp.mean(r, axis=-1, keepdims=True)
    d = r - mu
    var = jnp.mean(d * d, axis=-1, keepdims=True)
    return d * lax.rsqrt(var + LN_EPS) * g + b


def _swiglu(xb, wg_ref, wu_ref, wd_ref):
    g = _dot(xb, wg_ref[...])
    u = _dot(xb, wu_ref[...])
    h = (g * _sigmoid(g)) * u
    return _dot(h.astype(BF16), wd_ref[...])


def _const_spec(shape):
    nd = len(shape)
    return pl.BlockSpec(shape, lambda *_: (0,) * nd, pipeline_mode=pl.Buffered(1))


def _params(n_axes):
    return pltpu.CompilerParams(dimension_semantics=("arbitrary",) * n_axes,
                                vmem_limit_bytes=VMEM_LIMIT_BYTES)


def _row_tile(n):
    for tm in (512, 256, 128, 64, 32, 16, 8):
        if n % tm == 0:
            return tm
    raise ValueError(f"row count {n} is not a multiple of 8")


def _ffn_ln_kernel(x_ref, wg_ref, wu_ref, wd_ref, g_ref, b_ref, o_ref, *, alpha):
    x = x_ref[...]
    y = _swiglu(x.astype(BF16), wg_ref, wu_ref, wd_ref)
    o_ref[...] = _layer_norm(alpha * x + 0.5 * y, g_ref[...], b_ref[...])


def _ffn_ln(x, wg, wu, wd, g, b, alpha):
    n, d = x.shape
    tm = _row_tile(n)
    return pl.pallas_call(
        functools.partial(_ffn_ln_kernel, alpha=alpha),
        grid=(n // tm,),
        in_specs=[pl.BlockSpec((tm, d), lambda i: (i, 0)),
                  _const_spec(wg.shape), _const_spec(wu.shape), _const_spec(wd.shape),
                  _const_spec(g.shape), _const_spec(b.shape)],
        out_specs=pl.BlockSpec((tm, d), lambda i: (i, 0)),
        out_shape=jax.ShapeDtypeStruct((n, d), F32),
        compiler_params=_params(1),
        name="ffn_ln",
    )(x, wg, wu, wd, g, b)


def _out_ffn_ple_kernel(x1_ref, conv_ref, att_ref, p_ref, wo_ref, g2_ref, b2_ref,
                        wg_ref, wu_ref, wd_ref, g3_ref, b3_ref, pg_ref, pp_ref, o_ref,
                        *, alpha, d_conv):
    m = _dot(conv_ref[...], wo_ref[:d_conv, :]) + _dot(att_ref[...], wo_ref[d_conv:, :])
    x2 = _layer_norm(alpha * x1_ref[...] + m, g2_ref[...], b2_ref[...])
    y = _swiglu(x2.astype(BF16), wg_ref, wu_ref, wd_ref)
    x3 = _layer_norm(alpha * x2 + 0.5 * y, g3_ref[...], b3_ref[...])
    gate = _sigmoid(_dot(x3.astype(BF16), pg_ref[...]))
    o_ref[...] = x3 + gate * _dot(p_ref[...].astype(BF16), pp_ref[...])


def _out_ffn_ple(x1, conv, att, p, wo, g2, b2, wg, wu, wd, g3, b3, pg, pp, alpha):
    n, d = x1.shape
    tm = min(_row_tile(n), 256)
    rows = lambda a: pl.BlockSpec((tm, a.shape[1]), lambda i: (i, 0))
    consts = (wo, g2, b2, wg, wu, wd, g3, b3, pg, pp)
    return pl.pallas_call(
        functools.partial(_out_ffn_ple_kernel, alpha=alpha, d_conv=conv.shape[1]),
        grid=(n // tm,),
        in_specs=[rows(x1), rows(conv), rows(att), rows(p)] + [_const_spec(c.shape) for c in consts],
        out_specs=rows(x1),
        out_shape=jax.ShapeDtypeStruct((n, d), F32),
        compiler_params=_params(1),
        name="out_ffn_ple",
    )(x1, conv, att, p, *consts)


def _conv_branch_tail(y, cb_ref, cng_ref, cnb_ref):
    c = _layer_norm(y + cb_ref[...], cng_ref[...], cnb_ref[...])
    return c * _sigmoid(c)


def _inproj_prompt_kernel(x1_ref, w_in_ref, b_in_ref, cw_ref, cb_ref, cng_ref, cnb_ref,
                          k_ref, v_ref, conv_ref, utail_ref, kmean_ref, qT_ref, kaug_ref, vT_ref,
                          uext_ref, *, tm, n_heads, d_conv, d_att):
    t = pl.program_id(1)
    bpt = tm // BLOCK
    z = _dot(x1_ref[0].astype(BF16), w_in_ref[...]) + b_in_ref[...]
    a = z[:, :d_conv]
    g = z[:, d_conv:2 * d_conv]
    zq = z[:, 2 * d_conv:2 * d_conv + d_att]
    zk = z[:, 2 * d_conv + d_att:2 * d_conv + 2 * d_att]
    zv = z[:, 2 * d_conv + 2 * d_att:]
    k_ref[0] = zk
    v_ref[0] = zv

    @pl.when(t == 0)
    def _():
        uext_ref[0:CONV_HALO, :] = jnp.zeros((CONV_HALO, d_conv), F32)

    uext_ref[CONV_HALO:CONV_HALO + tm, :] = a * _sigmoid(g)
    off = CONV_HALO - (CONV_WIDTH - 1)
    for r0 in range(0, tm, CONV_ROWS):
        window = uext_ref[r0:r0 + CONV_ROWS + CONV_HALO, :]
        n_win = CONV_ROWS + CONV_HALO
        shifted = [window] + [pltpu.roll(window, n_win - s, 0) for s in range(1, SUBLANES)]
        y = jnp.zeros((CONV_ROWS, d_conv), F32)
        for j in range(CONV_WIDTH):
            s = (off + j) % SUBLANES
            base = off + j - s
            y = y + cw_ref[j:j + 1, :] * shifted[s][base:base + CONV_ROWS, :]
        conv_ref[0, r0:r0 + CONV_ROWS, :] = _conv_branch_tail(y, cb_ref, cng_ref, cnb_ref).astype(BF16)
    tail = uext_ref[tm:tm + CONV_HALO, :]
    utail_ref[0] = tail
    uext_ref[0:CONV_HALO, :] = tail

    @pl.when(t == 0)
    def _():
        kmean_ref[0] = jnp.zeros(kmean_ref.shape[1:], F32)

    km = kmean_ref[0]
    km_row = lax.broadcasted_iota(jnp.int32, km.shape, 0)
    for blk in range(bpt):
        mean = jnp.sum(zk[blk * BLOCK:(blk + 1) * BLOCK], axis=0, keepdims=True) * (1.0 / BLOCK)
        km = jnp.where(km_row == t * bpt + blk, mean, km)
    kmean_ref[0] = km

    zqT = (zq * (HEAD_DIM ** -0.5)).T
    zvT = zv.T
    lane = lax.broadcasted_iota(jnp.int32, (BLOCK, LANES), 1)
    pos = lax.broadcasted_iota(jnp.int32, (BLOCK, LANES), 0).astype(F32)
    col = lane - HEAD_DIM
    ones_row = jnp.where(lax.broadcasted_iota(jnp.int32, (V_ROWS - HEAD_DIM, BLOCK), 0) == 0, 1.0, 0.0)
    for blk in range(bpt):
        j = t * bpt + blk
        rows = slice(blk * BLOCK, (blk + 1) * BLOCK)
        for h in range(n_heads):
            slope = 2.0 ** (-8.0 * (h + 1) / n_heads)
            hd = slice(h * HEAD_DIM, (h + 1) * HEAD_DIM)
            qT_ref[0, h, blk] = zqT[hd, rows].astype(BF16)
            vT_ref[0, h, blk] = jnp.concatenate([zvT[hd, rows], ones_row], axis=0).astype(BF16)
            pair = zk[rows, (h // 2) * LANES:(h // 2 + 1) * LANES]
            if h % 2:
                pair = pltpu.roll(pair, HEAD_DIM, 1)
            extra = (jnp.where(col == j, 1.0, 0.0)
                     + jnp.where(col == SEL_ROWS, slope * pos, 0.0)
                     + jnp.where(col == SEL_ROWS + 1, (slope * BLOCK) * j.astype(F32), 0.0))
            kaug_ref[0, h, blk] = jnp.where(lane < HEAD_DIM, pair, extra).astype(BF16)


def _inproj_prompt(x1, w_in, b_in, cw, cb, cng, cnb, n_heads):
    bsz, t_len, d = x1.shape
    d_conv = cw.shape[1]
    d_att = n_heads * HEAD_DIM
    tm = 512 if t_len % 512 == 0 else BLOCK
    bpt = tm // BLOCK
    nb = t_len // BLOCK
    tiles = lambda w: pl.BlockSpec((1, tm, w), lambda b, t: (b, t, 0))
    per_seq = lambda r, w: pl.BlockSpec((1, r, w), lambda b, t: (b, 0, 0))
    head_blocks = lambda r, w: pl.BlockSpec((1, n_heads, bpt, r, w), lambda b, t: (b, 0, t, 0, 0))
    consts = (w_in, b_in, cw, cb, cng, cnb)
    return pl.pallas_call(
        functools.partial(_inproj_prompt_kernel, tm=tm, n_heads=n_heads, d_conv=d_conv, d_att=d_att),
        grid=(bsz, t_len // tm),
        in_specs=[tiles(d)] + [_const_spec(c.shape) for c in consts],
        out_specs=[tiles(d_att), tiles(d_att), tiles(d_conv), per_seq(CONV_HALO, d_conv), per_seq(nb, d_att),
                   head_blocks(HEAD_DIM, BLOCK), head_blocks(BLOCK, AUG), head_blocks(V_ROWS, BLOCK)],
        out_shape=[jax.ShapeDtypeStruct((bsz, t_len, d_att), F32),
                   jax.ShapeDtypeStruct((bsz, t_len, d_att), F32),
                   jax.ShapeDtypeStruct((bsz, t_len, d_conv), BF16),
                   jax.ShapeDtypeStruct((bsz, CONV_HALO, d_conv), F32),
                   jax.ShapeDtypeStruct((bsz, nb, d_att), F32),
                   jax.ShapeDtypeStruct((bsz, n_heads, nb, HEAD_DIM, BLOCK), BF16),
                   jax.ShapeDtypeStruct((bsz, n_heads, nb, BLOCK, AUG), BF16),
                   jax.ShapeDtypeStruct((bsz, n_heads, nb, V_ROWS, BLOCK), BF16)],
        scratch_shapes=[pltpu.VMEM((CONV_HALO + tm, d_conv), F32)],
        compiler_params=_params(2),
        name="inproj_prompt",
    )(x1, *consts)


def _topk_select(gate, idx, n_valid):
    n = gate.shape[0]
    valid = idx < n_valid
    gm = jnp.where(valid, gate, -jnp.inf)
    rank = jnp.zeros(gate.shape, jnp.int32)
    for m in range(n):
        row = gm[m:m + 1, :]
        rank = rank + jnp.where(row > gm, 1, jnp.where(row == gm, jnp.where(idx > m, 1, 0), 0))
    return (rank < TOPK) & valid


def _moba_prompt_kernel(qT_ref, kaug_ref, vT_ref, kmean_ref, o_ref, qaug_scr, s_scr, m_scr, acc_scr,
                        *, nb, n_heads):
    i = pl.program_id(1)
    blk_idx = lax.broadcasted_iota(jnp.int32, (nb, BLOCK), 0)
    causal = (lax.broadcasted_iota(jnp.int32, (BLOCK, BLOCK), 0)
              <= lax.broadcasted_iota(jnp.int32, (BLOCK, BLOCK), 1))
    aug_row = lax.broadcasted_iota(jnp.int32, (AUG - HEAD_DIM - SEL_ROWS, BLOCK), 0)
    bias_rows = jnp.where(aug_row < 2, 1.0, 0.0).astype(BF16)

    def absorb(h, st, j):
        m = m_scr[h]
        m_new = jnp.maximum(m, jnp.max(st, axis=0, keepdims=True))
        p = jnp.exp(st - m_new).astype(BF16)
        acc_scr[h] = acc_scr[h] * jnp.exp(m - m_new) + _dot(vT_ref[0, h, j], p)
        m_scr[h] = m_new

    gates = [_dot(kmean_ref[0, h].astype(BF16), qT_ref[0, h, 0]) for h in range(n_heads)]
    for h in range(n_heads):
        keep = _topk_select(gates[h], blk_idx, i) | (blk_idx == i)
        selbias = jnp.where(keep, 0.0, MASKED)
        if nb < SEL_ROWS:
            selbias = jnp.concatenate([selbias, jnp.zeros((SEL_ROWS - nb, BLOCK), F32)], axis=0)
        qaug_scr[h] = jnp.concatenate([qT_ref[0, h, 0], selbias.astype(BF16), bias_rows], axis=0)
        m_scr[h] = jnp.full((1, BLOCK), MASKED, F32)
        acc_scr[h] = jnp.zeros((V_ROWS, BLOCK), F32)
    for h in range(n_heads):
        s_scr[0, h] = _dot(kaug_ref[0, h, 0], qaug_scr[h])

    def past_block(j, cur):
        for h in range(n_heads):
            s_scr[1 - cur, h] = _dot(kaug_ref[0, h, j + 1], qaug_scr[h])
        for h in range(n_heads):
            absorb(h, s_scr[cur, h], j)

    def past_pair(t, carry):
        past_block(2 * t, 0)
        past_block(2 * t + 1, 1)
        return carry

    def own_block(cur):
        for h in range(n_heads):
            absorb(h, jnp.where(causal, s_scr[cur, h], MASKED), i)
        outs = [acc_scr[h, :HEAD_DIM, :] / acc_scr[h, HEAD_DIM:HEAD_DIM + 1, :] for h in range(n_heads)]
        o_ref[0] = jnp.concatenate(outs, axis=0).T.astype(BF16)

    lax.fori_loop(0, i // 2, past_pair, 0)

    @pl.when(i % 2 == 0)
    def _():
        own_block(0)

    @pl.when(i % 2 == 1)
    def _():
        past_block(i - 1, 0)
        own_block(1)


def _moba_prompt(qT, kaug, vT, kmean):
    bsz, n_heads, nb = qT.shape[:3]
    assert nb <= SEL_ROWS, "prompt sequences longer than SEL_ROWS key blocks are not supported"
    seq = lambda r, w: pl.BlockSpec((1, n_heads, nb, r, w), lambda b, i: (b, 0, 0, 0, 0))
    return pl.pallas_call(
        functools.partial(_moba_prompt_kernel, nb=nb, n_heads=n_heads),
        grid=(bsz, nb),
        in_specs=[pl.BlockSpec((1, n_heads, 1, HEAD_DIM, BLOCK), lambda b, i: (b, 0, i, 0, 0)),
                  seq(BLOCK, AUG), seq(V_ROWS, BLOCK),
                  pl.BlockSpec((1, n_heads, nb, HEAD_DIM), lambda b, i: (b, 0, 0, 0))],
        out_specs=pl.BlockSpec((1, BLOCK, n_heads * HEAD_DIM), lambda b, i: (b, i, 0)),
        out_shape=jax.ShapeDtypeStruct((bsz, nb * BLOCK, n_heads * HEAD_DIM), BF16),
        scratch_shapes=[pltpu.VMEM((n_heads, AUG, BLOCK), BF16),
                        pltpu.VMEM((2, n_heads, BLOCK, BLOCK), F32),
                        pltpu.VMEM((n_heads, 1, BLOCK), F32),
                        pltpu.VMEM((n_heads, V_ROWS, BLOCK), F32)],
        compiler_params=_params(2),
        name="moba_prompt",
    )(qT, kaug, vT, kmean)


def _inproj_sample_kernel(x1_ref, w_in_ref, b_in_ref, cw_ref, cb_ref, cng_ref, cnb_ref, state_ref,
                          q_ref, k_ref, v_ref, u_ref, conv_ref, *, bs, ts, d_conv, d_att):
    z = _dot(x1_ref[...].astype(BF16), w_in_ref[...]) + b_in_ref[...]
    a = z[:, :d_conv]
    g = z[:, d_conv:2 * d_conv]
    q_ref[...] = z[:, 2 * d_conv:2 * d_conv + d_att]
    k_ref[...] = z[:, 2 * d_conv + d_att:2 * d_conv + 2 * d_att]
    v_ref[...] = z[:, 2 * d_conv + 2 * d_att:]
    u_ref[...] = a * _sigmoid(g)
    n_state = CONV_WIDTH - 1
    for t in range(ts):
        y = jnp.zeros((bs, d_conv), F32)
        for j in range(CONV_WIDTH):
            tau = t + j
            if tau < n_state:
                slab = state_ref[tau]
            else:
                slab = u_ref[(tau - n_state) * bs:(tau - n_state + 1) * bs, :]
            y = y + cw_ref[j:j + 1, :] * slab
        conv_ref[t * bs:(t + 1) * bs, :] = _conv_branch_tail(y, cb_ref, cng_ref, cnb_ref).astype(BF16)


def _inproj_sample(x1, w_in, b_in, cw, cb, cng, cnb, state_tm, n_heads):
    n, d = x1.shape
    bs = state_tm.shape[1]
    ts = n // bs
    d_conv = cw.shape[1]
    d_att = n_heads * HEAD_DIM
    ins = (x1, w_in, b_in, cw, cb, cng, cnb, state_tm)
    out = lambda w, dt: jax.ShapeDtypeStruct((n, w), dt)
    outs = [out(d_att, F32), out(d_att, F32), out(d_att, F32), out(d_conv, F32), out(d_conv, BF16)]
    return pl.pallas_call(
        functools.partial(_inproj_sample_kernel, bs=bs, ts=ts, d_conv=d_conv, d_att=d_att),
        grid=(1,),
        in_specs=[_const_spec(a.shape) for a in ins],
        out_specs=[pl.BlockSpec(o.shape, lambda i: (0, 0)) for o in outs],
        out_shape=outs,
        compiler_params=_params(1),
        name="inproj_sample",
    )(*ins)


def _head_slopes(shape, n_heads, rows_per_head):
    head = lax.div(lax.broadcasted_iota(jnp.int32, shape, 0), jnp.int32(rows_per_head))
    out = jnp.zeros(shape, F32)
    for h in range(n_heads):
        out = jnp.where(head == h, 2.0 ** (-8.0 * (h + 1) / n_heads), out)
    return out


def _moba_sample_kernel(pt_ref, q_ref, kn_ref, vn_ref, *refs, pages, n_heads, ts, nbs):
    del pt_ref
    kp = refs[:pages]
    vp = refs[pages:2 * pages]
    o_ref = refs[2 * pages]
    kmean_scr, s_scr, shift_scr, mx_scr, l_scr, acc_scr, qbd_scr = refs[2 * pages + 1:]
    c = pl.program_id(1)
    n_k = pl.num_programs(1) // 2
    d_att = n_heads * HEAD_DIM
    rows = n_heads * ts
    page = kp[0].shape[-1]
    bpc = pages * page // BLOCK
    ppb = BLOCK // page

    @pl.when(c == 0)
    def _():
        qs = q_ref[0] * (HEAD_DIM ** -0.5)
        qrep = jnp.concatenate([qs] * n_heads, axis=0)
        row_head = lax.div(lax.broadcasted_iota(jnp.int32, (rows, d_att), 0), jnp.int32(ts))
        lane_head = lax.div(lax.broadcasted_iota(jnp.int32, (rows, d_att), 1), jnp.int32(HEAD_DIM))
        qbd_scr[...] = jnp.where(row_head == lane_head, qrep, 0.0).astype(BF16)

    qbd = qbd_scr[...]
    slope = _head_slopes((rows, 1), n_heads, ts)
    blk_lane = lax.broadcasted_iota(jnp.int32, (rows, nbs), 1)

    def block_of(page_refs, blk):
        return jnp.concatenate([page_refs[blk * ppb + i][0, 0].reshape(d_att, page) for i in range(ppb)], axis=1)

    def own_scores():
        pad = jnp.zeros((16 - ts, d_att), F32)
        kn = jnp.concatenate([kn_ref[0], pad], axis=0).astype(BF16)
        u_idx = lax.broadcasted_iota(jnp.int32, (rows, 16), 1)
        q_idx = lax.rem(lax.broadcasted_iota(jnp.int32, (rows, 16), 0), jnp.int32(ts))
        so = _dot_nt(qbd, kn) + slope * u_idx.astype(F32)
        return jnp.where(u_idx <= q_idx, so, MASKED)

    @pl.when(c < n_k)
    def _():
        @pl.when(c == 0)
        def _():
            kmean_scr[...] = jnp.zeros(kmean_scr.shape, F32)

        key_bias = slope * lax.broadcasted_iota(jnp.int32, (rows, BLOCK), 1).astype(F32)
        km_lane = lax.broadcasted_iota(jnp.int32, kmean_scr.shape, 1)
        for blk in range(bpc):
            n = c * bpc + blk
            kt = block_of(kp, blk)
            kmean = jnp.sum(kt, axis=1, keepdims=True) * (1.0 / BLOCK)
            kmean_scr[...] = jnp.where(km_lane == n, kmean, kmean_scr[...])
            s_scr[n] = _dot(qbd, kt.astype(BF16)) + key_bias

    @pl.when(c == n_k - 1)
    def _():
        gate = _dot(qbd, kmean_scr[...].astype(BF16))[:, :nbs]
        rank = jnp.zeros((rows, nbs), jnp.int32)
        for m in range(nbs):
            colm = gate[:, m:m + 1]
            rank = rank + jnp.where(colm > gate, 1, jnp.where(colm == gate, jnp.where(blk_lane > m, 1, 0), 0))
        sel = rank < TOPK
        blk_bias = slope * ((blk_lane - nbs) * BLOCK).astype(F32)
        blk_max = jnp.full((rows, nbs), MASKED, F32)
        for n in range(nbs):
            blk_max = jnp.where(blk_lane == n, jnp.max(s_scr[n], axis=1, keepdims=True), blk_max)
        e = jnp.where(sel, blk_max + blk_bias, MASKED)
        mx = jnp.maximum(jnp.max(e, axis=1, keepdims=True), jnp.max(own_scores(), axis=1, keepdims=True))
        mx_scr[...] = mx
        shift_scr[...] = jnp.where(sel, blk_bias - mx, MASKED)
        l_scr[...] = jnp.zeros(l_scr.shape, F32)
        acc_scr[...] = jnp.zeros(acc_scr.shape, F32)

    @pl.when(c >= n_k)
    def _():
        shift = shift_scr[...]
        zero_rows = jnp.zeros((LANES - rows, BLOCK), F32)
        for blk in range(bpc):
            n = (c - n_k) * bpc + blk
            sh = jnp.sum(jnp.where(blk_lane == n, shift, 0.0), axis=1, keepdims=True)
            p = jnp.exp(s_scr[n] + sh)
            l_scr[...] += jnp.sum(p, axis=1, keepdims=True)
            pT = jnp.concatenate([p, zero_rows], axis=0).T.astype(BF16)
            acc_scr[...] += _dot(block_of(vp, blk).astype(BF16), pT)

    @pl.when(c == pl.num_programs(1) - 1)
    def _():
        po = jnp.exp(own_scores() - mx_scr[...])
        pad = jnp.zeros((16 - ts, d_att), F32)
        vn = jnp.concatenate([vn_ref[0], pad], axis=0).astype(BF16)
        num = acc_scr[...].T[:rows] + _dot(po.astype(BF16), vn)
        res = num / (l_scr[...] + jnp.sum(po, axis=1, keepdims=True))
        o_ref[0] = jnp.concatenate([res[h * ts:(h + 1) * ts, h * HEAD_DIM:(h + 1) * HEAD_DIM]
                                    for h in range(n_heads)], axis=-1)


def _moba_sample(q, k_new, v_new, page_table, cache_k, cache_v, layer):
    bs, ts, d_att = q.shape
    n_pages = page_table.shape[1]
    n_heads, _, page = cache_k.shape[2:]
    assert BLOCK % page == 0 and (n_pages * page) % BLOCK == 0 and ts <= 8
    nbs = n_pages * page // BLOCK
    rows = n_heads * ts
    assert nbs <= LANES and rows <= LANES
    ppb = BLOCK // page
    pages = next(p for p in (16, 8, 4, 2, 1) if n_pages % p == 0 and p % ppb == 0)
    n_k = n_pages // pages
    pt = page_table.reshape(-1).astype(jnp.int32)
    seq = pl.BlockSpec((1, ts, d_att), lambda b, c, pt: (b, 0, 0))

    def k_page(p):
        return pl.BlockSpec((1, 1, n_heads, HEAD_DIM, page),
                            lambda b, c, pt: (layer, pt[b * n_pages + jnp.minimum(c, n_k - 1) * pages + p], 0, 0, 0))

    def v_page(p):
        return pl.BlockSpec((1, 1, n_heads, HEAD_DIM, page),
                            lambda b, c, pt: (layer, pt[b * n_pages + jnp.maximum(c - n_k, 0) * pages + p], 0, 0, 0))

    grid_spec = pltpu.PrefetchScalarGridSpec(
        num_scalar_prefetch=1,
        grid=(bs, 2 * n_k),
        in_specs=[seq, seq, seq] + [k_page(p) for p in range(pages)] + [v_page(p) for p in range(pages)],
        out_specs=seq,
        scratch_shapes=[pltpu.VMEM((d_att, LANES), F32),
                        pltpu.VMEM((nbs, rows, BLOCK), F32),
                        pltpu.VMEM((rows, nbs), F32),
                        pltpu.VMEM((rows, 1), F32),
                        pltpu.VMEM((rows, 1), F32),
                        pltpu.VMEM((d_att, LANES), F32),
                        pltpu.VMEM((rows, d_att), BF16)],
    )
    return pl.pallas_call(
        functools.partial(_moba_sample_kernel, pages=pages, n_heads=n_heads, ts=ts, nbs=nbs),
        grid_spec=grid_spec,
        out_shape=jax.ShapeDtypeStruct((bs, ts, d_att), F32),
        compiler_params=_params(2),
        name="moba_sample",
    )(pt, q, k_new, v_new, *([cache_k] * pages), *([cache_v] * pages))


def kernel(x_prompt, x_sample, cache_k, cache_v, state_conv, page_table, p_prompt, p_sample,
           ffn1_w_gate, ffn1_w_up, ffn1_w_down, ln1_g, ln1_b, w_in, b_in, conv_w, conv_b,
           conv_norm_g, conv_norm_b, w_out, ln2_g, ln2_b, ffn2_w_gate, ffn2_w_up, ffn2_w_down,
           ln3_g, ln3_b, ple_w_proj, ple_w_gate):
    depth = ffn1_w_gate.shape[0]
    alpha = (2.0 * depth) ** 0.25
    bsz, t_len, d = x_prompt.shape
    bs, ts, _ = x_sample.shape
    n_heads = cache_k.shape[3]
    assert cache_k.shape[4] == HEAD_DIM and conv_w.shape[1] == CONV_WIDTH
    assert n_heads == 8, "ALiBi slopes must be powers of two for the exact bf16 bias columns"
    assert t_len % BLOCK == 0
    d_att = n_heads * HEAD_DIM
    d_conv = conv_w.shape[2]
    n_state = CONV_WIDTH - 1
    row = lambda a: a.reshape(1, -1)

    cache_kt = cache_k.transpose(0, 1, 3, 4, 2)
    cache_vt = cache_v.transpose(0, 1, 3, 4, 2)
    yp = x_prompt.reshape(bsz * t_len, d)
    ys = x_sample.transpose(1, 0, 2).reshape(ts * bs, d)
    outs = [[] for _ in range(6)]
    for i in range(depth):
        f1 = (ffn1_w_gate[i].astype(BF16), ffn1_w_up[i].astype(BF16), ffn1_w_down[i].astype(BF16),
              row(ln1_g[i]), row(ln1_b[i]))
        mix = (w_in[i].astype(BF16), row(b_in[i]), conv_w[i], row(conv_b[i]),
               row(conv_norm_g[i]), row(conv_norm_b[i]))
        tail = (w_out[i].astype(BF16), row(ln2_g[i]), row(ln2_b[i]),
                ffn2_w_gate[i].astype(BF16), ffn2_w_up[i].astype(BF16), ffn2_w_down[i].astype(BF16),
                row(ln3_g[i]), row(ln3_b[i]), ple_w_gate[i].astype(BF16), ple_w_proj[i].astype(BF16))

        x1 = _ffn_ln(yp, *f1, alpha)
        k, v, conv, utail, kmean, qT, kaug, vT = _inproj_prompt(x1.reshape(bsz, t_len, d), *mix, n_heads)
        nb = t_len // BLOCK
        kmean_h = kmean.reshape(bsz, nb, n_heads, HEAD_DIM).transpose(0, 2, 1, 3)
        att = _moba_prompt(qT, kaug, vT, kmean_h)
        yp = _out_ffn_ple(x1, conv.reshape(-1, d_conv), att.reshape(-1, d_att),
                          p_prompt[i].reshape(bsz * t_len, -1), *tail, alpha)
        outs[0].append(k.reshape(bsz, t_len, n_heads, HEAD_DIM))
        outs[1].append(v.reshape(bsz, t_len, n_heads, HEAD_DIM))
        outs[2].append(utail[:, CONV_HALO - n_state:])

        x1 = _ffn_ln(ys, *f1, alpha)
        state = state_conv[i]
        q, k, v, u, conv = _inproj_sample(x1, *mix, state.transpose(1, 0, 2), n_heads)
        to_bm = lambda a: a.reshape(ts, bs, -1).transpose(1, 0, 2)
        q, k, v, u = to_bm(q), to_bm(k), to_bm(v), to_bm(u)
        att = _moba_sample(q, k, v, page_table, cache_kt, cache_vt, i)
        att = att.transpose(1, 0, 2).reshape(ts * bs, d_att).astype(BF16)
        ys = _out_ffn_ple(x1, conv, att, p_sample[i].transpose(1, 0, 2).reshape(ts * bs, -1), *tail, alpha)
        outs[3].append(k.reshape(bs, ts, n_heads, HEAD_DIM))
        outs[4].append(v.reshape(bs, ts, n_heads, HEAD_DIM))
        outs[5].append(jnp.concatenate([state, u], axis=1)[:, ts:])

    return (yp.reshape(bsz, t_len, d), ys.reshape(ts, bs, d).transpose(1, 0, 2),
            *(jnp.stack(o) for o in outs))
```

```python
import functools

import jax
import jax.numpy as jnp
import numpy as np
from jax import lax
from jax.experimental import pallas as pl
from jax.experimental.pallas import tpu as pltpu

F32 = jnp.float32
BF16 = jnp.bfloat16

LN_EPS = 1e-5
HEAD_DIM = 64
BLOCK = 256
TOPK = 3
CONV_WIDTH = 31
CONV_HALO = 32
CONV_ROWS = 32
SUBLANES = 8
MASKED = -(2.0 ** 100)
SEL_ROWS = 16
BIAS_PARTS = 4
AUG = 128
LOG2E = 1.4426950408889634
V_ROWS = 80
LANES = 128
VMEM_LIMIT_BYTES = 56 * 1024 * 1024


def _dot(a, b):
    return jnp.dot(a, b, preferred_element_type=F32)


def _dot_nt(a, b):
    return lax.dot_general(a, b, (((1,), (1,)), ((), ())), preferred_element_type=F32)


def _sigmoid(x):
    return 1.0 / (1.0 + jnp.exp(-x))


def _layer_norm(r, g, b):
    mu = jnp.mean(r, axis=-1, keepdims=True)
    d = r - mu
    var = jnp.mean(d * d, axis=-1, keepdims=True)
    return d * lax.rsqrt(var + LN_EPS) * g + b


def _swiglu(xb, wg_ref, wu_ref, wd_ref, chunks=1):
    d_ff = wg_ref.shape[1]
    step = d_ff // chunks
    assert step * chunks == d_ff and step % LANES == 0
    y = None
    for c in range(chunks):
        cols = slice(c * step, (c + 1) * step)
        g = _dot(xb, wg_ref[:, cols])
        u = _dot(xb, wu_ref[:, cols])
        part = _dot(((g * _sigmoid(g)) * u).astype(BF16), wd_ref[cols, :])
        y = part if y is None else y + part
    return y


def _const_spec(shape):
    nd = len(shape)
    return pl.BlockSpec(shape, lambda *_: (0,) * nd, pipeline_mode=pl.Buffered(1))


def _params(n_axes):
    return pltpu.CompilerParams(dimension_semantics=("arbitrary",) * n_axes,
                                vmem_limit_bytes=VMEM_LIMIT_BYTES)


def _row_tile(n):
    for tm in (512, 256, 128, 64, 32, 16, 8):
        if n % tm == 0:
            return tm
    raise ValueError(f"row count {n} is not a multiple of 8")


def _ffn_ln_kernel(x_ref, wg_ref, wu_ref, wd_ref, g_ref, b_ref, o_ref, *, alpha):
    x = x_ref[...]
    y = _swiglu(x.astype(BF16), wg_ref, wu_ref, wd_ref)
    o_ref[...] = _layer_norm(alpha * x + 0.5 * y, g_ref[...], b_ref[...])


def _ffn_ln(x, wg, wu, wd, g, b, alpha):
    n, d = x.shape
    tm = _row_tile(n)
    return pl.pallas_call(
        functools.partial(_ffn_ln_kernel, alpha=alpha),
        grid=(n // tm,),
        in_specs=[pl.BlockSpec((tm, d), lambda i: (i, 0)),
                  _const_spec(wg.shape), _const_spec(wu.shape), _const_spec(wd.shape),
                  _const_spec(g.shape), _const_spec(b.shape)],
        out_specs=pl.BlockSpec((tm, d), lambda i: (i, 0)),
        out_shape=jax.ShapeDtypeStruct((n, d), F32),
        compiler_params=_params(1),
        name="ffn_ln",
    )(x, wg, wu, wd, g, b)


def _out_ffn_ple_kernel(x1_ref, conv_ref, att_ref, p_ref, wo_ref, g2_ref, b2_ref,
                        wg_ref, wu_ref, wd_ref, g3_ref, b3_ref, pg_ref, pp_ref, o_ref,
                        *, alpha, d_conv):
    m = _dot(conv_ref[...], wo_ref[:d_conv, :]) + _dot(att_ref[...], wo_ref[d_conv:, :])
    x2 = _layer_norm(alpha * x1_ref[...] + m, g2_ref[...], b2_ref[...])
    y = _swiglu(x2.astype(BF16), wg_ref, wu_ref, wd_ref)
    x3 = _layer_norm(alpha * x2 + 0.5 * y, g3_ref[...], b3_ref[...])
    gate = _sigmoid(_dot(x3.astype(BF16), pg_ref[...]))
    o_ref[...] = x3 + gate * _dot(p_ref[...].astype(BF16), pp_ref[...])


def _out_ffn_ple(x1, conv, att, p, wo, g2, b2, wg, wu, wd, g3, b3, pg, pp, alpha):
    n, d = x1.shape
    tm = _row_tile(n)
    rows = lambda a: pl.BlockSpec((tm, a.shape[1]), lambda i: (i, 0))
    consts = (wo, g2, b2, wg, wu, wd, g3, b3, pg, pp)
    return pl.pallas_call(
        functools.partial(_out_ffn_ple_kernel, alpha=alpha, d_conv=conv.shape[1]),
        grid=(n // tm,),
        in_specs=[rows(x1), rows(conv), rows(att), rows(p)] + [_const_spec(c.shape) for c in consts],
        out_specs=rows(x1),
        out_shape=jax.ShapeDtypeStruct((n, d), F32),
        compiler_params=_params(1),
        name="out_ffn_ple",
    )(x1, conv, att, p, *consts)


def _conv_branch_tail(y, cb_ref, cng_ref, cnb_ref):
    c = _layer_norm(y + cb_ref[...], cng_ref[...], cnb_ref[...])
    return c * _sigmoid(c)


def _inproj_prompt_kernel(x1_ref, pos_tab_ref, blk_tab_ref, w_in_ref, b_in_ref, cw_ref, cb_ref, cng_ref, cnb_ref,
                          k_ref, v_ref, conv_ref, utail_ref, kmean_ref, qT_ref, kaug_ref, vT_ref,
                          uext_ref, *, tm, n_heads, d_conv, d_att):
    t = pl.program_id(1)
    bpt = tm // BLOCK
    z = _dot(x1_ref[0].astype(BF16), w_in_ref[...]) + b_in_ref[...]
    a = z[:, :d_conv]
    g = z[:, d_conv:2 * d_conv]
    zq = z[:, 2 * d_conv:2 * d_conv + d_att]
    zk = z[:, 2 * d_conv + d_att:2 * d_conv + 2 * d_att]
    zv = z[:, 2 * d_conv + 2 * d_att:]
    k_ref[0] = zk
    v_ref[0] = zv

    @pl.when(t == 0)
    def _():
        uext_ref[0:CONV_HALO, :] = jnp.zeros((CONV_HALO, d_conv), F32)

    uext_ref[CONV_HALO:CONV_HALO + tm, :] = a * _sigmoid(g)
    off = CONV_HALO - (CONV_WIDTH - 1)
    for r0 in range(0, tm, CONV_ROWS):
        window = uext_ref[r0:r0 + CONV_ROWS + CONV_HALO, :]
        n_win = CONV_ROWS + CONV_HALO
        shifted = [window] + [pltpu.roll(window, n_win - s, 0) for s in range(1, SUBLANES)]
        y = jnp.zeros((CONV_ROWS, d_conv), F32)
        for j in range(CONV_WIDTH):
            s = (off + j) % SUBLANES
            base = off + j - s
            y = y + cw_ref[j:j + 1, :] * shifted[s][base:base + CONV_ROWS, :]
        conv_ref[0, r0:r0 + CONV_ROWS, :] = _conv_branch_tail(y, cb_ref, cng_ref, cnb_ref).astype(BF16)
    tail = uext_ref[tm:tm + CONV_HALO, :]
    utail_ref[0] = tail
    uext_ref[0:CONV_HALO, :] = tail

    @pl.when(t == 0)
    def _():
        kmean_ref[0] = jnp.zeros(kmean_ref.shape[1:], F32)

    km = kmean_ref[0]
    km_row = lax.broadcasted_iota(jnp.int32, km.shape, 0)
    for blk in range(bpt):
        mean = jnp.sum(zk[blk * BLOCK:(blk + 1) * BLOCK], axis=0, keepdims=True) * (1.0 / BLOCK)
        km = jnp.where(km_row == t * bpt + blk, mean, km)
    kmean_ref[0] = km

    zqT = (zq * (HEAD_DIM ** -0.5 * LOG2E)).T
    zvT = zv.T
    lane = lax.broadcasted_iota(jnp.int32, (BLOCK, LANES), 1)
    ones_row = jnp.where(lax.broadcasted_iota(jnp.int32, (V_ROWS - HEAD_DIM, BLOCK), 0) == 0, 1.0, 0.0)
    for blk in range(bpt):
        rows = slice(blk * BLOCK, (blk + 1) * BLOCK)
        for h in range(n_heads):
            hd = slice(h * HEAD_DIM, (h + 1) * HEAD_DIM)
            qT_ref[0, h, blk] = zqT[hd, rows].astype(BF16)
            vT_ref[0, h, blk] = jnp.concatenate([zvT[hd, rows], ones_row], axis=0).astype(BF16)
            pair = zk[rows, (h // 2) * LANES:(h // 2 + 1) * LANES]
            if h % 2:
                pair = pltpu.roll(pair, HEAD_DIM, 1)
            extra = pos_tab_ref[h] + blk_tab_ref[0, h, blk:blk + 1, :]
            kaug_ref[0, h, blk] = jnp.where(lane < HEAD_DIM, pair, extra).astype(BF16)


def _split_bf16(x, parts):
    out = []
    for _ in range(parts):
        piece = x.astype(jnp.bfloat16).astype(np.float64)
        out.append(piece.astype(np.float32))
        x = x - piece
    return out


def _alibi_tables(n_heads, nb, bpt):
    slopes = 2.0 ** (-8.0 * (np.arange(n_heads, dtype=np.float64) + 1.0) / n_heads) * LOG2E
    c0 = HEAD_DIM + SEL_ROWS
    pos_tab = np.zeros((n_heads, BLOCK, AUG), np.float32)
    for k, piece in enumerate(_split_bf16(slopes[:, None] * np.arange(BLOCK, dtype=np.float64)[None, :], BIAS_PARTS)):
        pos_tab[:, :, c0 + k] = piece
    blk_tab = np.zeros((nb, n_heads, AUG), np.float32)
    blk_tab[np.arange(nb), :, HEAD_DIM + np.arange(nb)] = 1.0
    blk_bias = slopes[None, :] * (BLOCK * np.arange(nb, dtype=np.float64))[:, None]
    for k, piece in enumerate(_split_bf16(blk_bias, BIAS_PARTS)):
        blk_tab[:, :, c0 + BIAS_PARTS + k] = piece
    blk_tab = blk_tab.reshape(nb // bpt, bpt, n_heads, AUG).transpose(0, 2, 1, 3)
    return jnp.asarray(pos_tab), jnp.asarray(blk_tab)


def _inproj_prompt(x1, w_in, b_in, cw, cb, cng, cnb, n_heads):
    bsz, t_len, d = x1.shape
    d_conv = cw.shape[1]
    d_att = n_heads * HEAD_DIM
    tm = 512 if t_len % 512 == 0 else BLOCK
    bpt = tm // BLOCK
    nb = t_len // BLOCK
    tiles = lambda w: pl.BlockSpec((1, tm, w), lambda b, t: (b, t, 0))
    per_seq = lambda r, w: pl.BlockSpec((1, r, w), lambda b, t: (b, 0, 0))
    head_blocks = lambda r, w: pl.BlockSpec((1, n_heads, bpt, r, w), lambda b, t: (b, 0, t, 0, 0))
    pos_tab, blk_tab = _alibi_tables(n_heads, nb, bpt)
    consts = (w_in, b_in, cw, cb, cng, cnb)
    return pl.pallas_call(
        functools.partial(_inproj_prompt_kernel, tm=tm, n_heads=n_heads, d_conv=d_conv, d_att=d_att),
        grid=(bsz, t_len // tm),
        in_specs=[tiles(d), _const_spec(pos_tab.shape),
                  pl.BlockSpec((1, n_heads, bpt, AUG), lambda b, t: (t, 0, 0, 0))]
                 + [_const_spec(c.shape) for c in consts],
        out_specs=[tiles(d_att), tiles(d_att), tiles(d_conv), per_seq(CONV_HALO, d_conv), per_seq(nb, d_att),
                   head_blocks(HEAD_DIM, BLOCK), head_blocks(BLOCK, AUG), head_blocks(V_ROWS, BLOCK)],
        out_shape=[jax.ShapeDtypeStruct((bsz, t_len, d_att), F32),
                   jax.ShapeDtypeStruct((bsz, t_len, d_att), F32),
                   jax.ShapeDtypeStruct((bsz, t_len, d_conv), BF16),
                   jax.ShapeDtypeStruct((bsz, CONV_HALO, d_conv), F32),
                   jax.ShapeDtypeStruct((bsz, nb, d_att), F32),
                   jax.ShapeDtypeStruct((bsz, n_heads, nb, HEAD_DIM, BLOCK), BF16),
                   jax.ShapeDtypeStruct((bsz, n_heads, nb, BLOCK, AUG), BF16),
                   jax.ShapeDtypeStruct((bsz, n_heads, nb, V_ROWS, BLOCK), BF16)],
        scratch_shapes=[pltpu.VMEM((CONV_HALO + tm, d_conv), F32)],
        compiler_params=_params(2),
        name="inproj_prompt",
    )(x1, pos_tab, blk_tab, *consts)


def _topk_select(gate, idx, n_valid):
    n = gate.shape[0]
    valid = idx < n_valid
    gm = jnp.where(valid, gate, -jnp.inf)
    rank = jnp.zeros(gate.shape, jnp.int32)
    for m in range(n):
        row = gm[m:m + 1, :]
        rank = rank + jnp.where(row > gm, 1, jnp.where(row == gm, jnp.where(idx > m, 1, 0), 0))
    return (rank < TOPK) & valid


def _moba_prompt_kernel(qT_ref, kaug_ref, vT_ref, kmean_ref, o_ref, qaug_scr, s_scr, m_scr, acc_scr,
                        *, nb, n_heads):
    i = pl.program_id(1)
    blk_idx = lax.broadcasted_iota(jnp.int32, (nb, BLOCK), 0)
    causal = (lax.broadcasted_iota(jnp.int32, (BLOCK, BLOCK), 0)
              <= lax.broadcasted_iota(jnp.int32, (BLOCK, BLOCK), 1))
    aug_row = lax.broadcasted_iota(jnp.int32, (AUG - HEAD_DIM - SEL_ROWS, BLOCK), 0)
    bias_rows = jnp.where(aug_row < 2 * BIAS_PARTS, 1.0, 0.0).astype(BF16)

    def absorb(h, st, j):
        m = m_scr[h]
        m_new = jnp.maximum(m, jnp.max(st, axis=0, keepdims=True))
        p = jnp.exp2(st - m_new).astype(BF16)
        acc_scr[h] = acc_scr[h] * jnp.exp2(m - m_new) + _dot(vT_ref[0, h, j], p)
        m_scr[h] = m_new

    gates = [_dot(kmean_ref[0, h].astype(BF16), qT_ref[0, h, 0]) for h in range(n_heads)]
    for h in range(n_heads):
        keep = _topk_select(gates[h], blk_idx, i) | (blk_idx == i)
        selbias = jnp.where(keep, 0.0, MASKED)
        if nb < SEL_ROWS:
            selbias = jnp.concatenate([selbias, jnp.zeros((SEL_ROWS - nb, BLOCK), F32)], axis=0)
        qaug_scr[h] = jnp.concatenate([qT_ref[0, h, 0], selbias.astype(BF16), bias_rows], axis=0)
        m_scr[h] = jnp.full((1, BLOCK), MASKED, F32)
        acc_scr[h] = jnp.zeros((V_ROWS, BLOCK), F32)
    for h in range(n_heads):
        s_scr[0, h] = _dot(kaug_ref[0, h, 0], qaug_scr[h])

    def past_block(j, cur):
        for h in range(n_heads):
            s_scr[1 - cur, h] = _dot(kaug_ref[0, h, j + 1], qaug_scr[h])
        for h in range(n_heads):
            absorb(h, s_scr[cur, h], j)

    def past_pair(t, carry):
        past_block(2 * t, 0)
        past_block(2 * t + 1, 1)
        return carry

    def own_block(cur):
        for h in range(n_heads):
            absorb(h, jnp.where(causal, s_scr[cur, h], MASKED), i)
        outs = [acc_scr[h, :HEAD_DIM, :] / acc_scr[h, HEAD_DIM:HEAD_DIM + 1, :] for h in range(n_heads)]
        o_ref[0] = jnp.concatenate(outs, axis=0).T.astype(BF16)

    lax.fori_loop(0, i // 2, past_pair, 0)

    @pl.when(i % 2 == 0)
    def _():
        own_block(0)

    @pl.when(i % 2 == 1)
    def _():
        past_block(i - 1, 0)
        own_block(1)


def _moba_prompt(qT, kaug, vT, kmean):
    bsz, n_heads, nb = qT.shape[:3]
    assert nb <= SEL_ROWS, "prompt sequences longer than SEL_ROWS key blocks are not supported"
    seq = lambda r, w: pl.BlockSpec((1, n_heads, nb, r, w), lambda b, i: (b, 0, 0, 0, 0))
    return pl.pallas_call(
        functools.partial(_moba_prompt_kernel, nb=nb, n_heads=n_heads),
        grid=(bsz, nb),
        in_specs=[pl.BlockSpec((1, n_heads, 1, HEAD_DIM, BLOCK), lambda b, i: (b, 0, i, 0, 0)),
                  seq(BLOCK, AUG), seq(V_ROWS, BLOCK),
                  pl.BlockSpec((1, n_heads, nb, HEAD_DIM), lambda b, i: (b, 0, 0, 0))],
        out_specs=pl.BlockSpec((1, BLOCK, n_heads * HEAD_DIM), lambda b, i: (b, i, 0)),
        out_shape=jax.ShapeDtypeStruct((bsz, nb * BLOCK, n_heads * HEAD_DIM), BF16),
        scratch_shapes=[pltpu.VMEM((n_heads, AUG, BLOCK), BF16),
                        pltpu.VMEM((2, n_heads, BLOCK, BLOCK), F32),
                        pltpu.VMEM((n_heads, 1, BLOCK), F32),
                        pltpu.VMEM((n_heads, V_ROWS, BLOCK), F32)],
        compiler_params=_params(2),
        name="moba_prompt",
    )(qT, kaug, vT, kmean)


def _inproj_sample_kernel(x1_ref, w_in_ref, b_in_ref, cw_ref, cb_ref, cng_ref, cnb_ref, state_ref,
                          q_ref, k_ref, v_ref, u_ref, conv_ref, *, bs, ts, d_conv, d_att):
    z = _dot(x1_ref[...].astype(BF16), w_in_ref[...]) + b_in_ref[...]
    a = z[:, :d_conv]
    g = z[:, d_conv:2 * d_conv]
    q_ref[...] = z[:, 2 * d_conv:2 * d_conv + d_att]
    k_ref[...] = z[:, 2 * d_conv + d_att:2 * d_conv + 2 * d_att]
    v_ref[...] = z[:, 2 * d_conv + 2 * d_att:]
    u_ref[...] = a * _sigmoid(g)
    n_state = CONV_WIDTH - 1
    for t in range(ts):
        y = jnp.zeros((bs, d_conv), F32)
        for j in range(CONV_WIDTH):
            tau = t + j
            if tau < n_state:
                slab = state_ref[tau]
            else:
                slab = u_ref[(tau - n_state) * bs:(tau - n_state + 1) * bs, :]
            y = y + cw_ref[j:j + 1, :] * slab
        conv_ref[t * bs:(t + 1) * bs, :] = _conv_branch_tail(y, cb_ref, cng_ref, cnb_ref).astype(BF16)


def _inproj_sample(x1, w_in, b_in, cw, cb, cng, cnb, state_tm, n_heads):
    n, d = x1.shape
    bs = state_tm.shape[1]
    ts = n // bs
    d_conv = cw.shape[1]
    d_att = n_heads * HEAD_DIM
    ins = (x1, w_in, b_in, cw, cb, cng, cnb, state_tm)
    out = lambda w, dt: jax.ShapeDtypeStruct((n, w), dt)
    outs = [out(d_att, F32), out(d_att, F32), out(d_att, F32), out(d_conv, F32), out(d_conv, BF16)]
    return pl.pallas_call(
        functools.partial(_inproj_sample_kernel, bs=bs, ts=ts, d_conv=d_conv, d_att=d_att),
        grid=(1,),
        in_specs=[_const_spec(a.shape) for a in ins],
        out_specs=[pl.BlockSpec(o.shape, lambda i: (0, 0)) for o in outs],
        out_shape=outs,
        compiler_params=_params(1),
        name="inproj_sample",
    )(*ins)


def _head_slopes(shape, n_heads, rows_per_head):
    head = lax.div(lax.broadcasted_iota(jnp.int32, shape, 0), jnp.int32(rows_per_head))
    out = jnp.zeros(shape, F32)
    for h in range(n_heads):
        out = jnp.where(head == h, 2.0 ** (-8.0 * (h + 1) / n_heads), out)
    return out


def _moba_sample_kernel(pt_ref, q_ref, kn_ref, vn_ref, *refs, pages, n_heads, ts, nbs):
    del pt_ref
    kp = refs[:pages]
    vp = refs[pages:2 * pages]
    o_ref = refs[2 * pages]
    kmean_scr, s_scr, shift_scr, mx_scr, l_scr, acc_scr, qbd_scr = refs[2 * pages + 1:]
    c = pl.program_id(1)
    n_k = pl.num_programs(1) // 2
    d_att = n_heads * HEAD_DIM
    rows = n_heads * ts
    page = kp[0].shape[-1]
    bpc = pages * page // BLOCK
    ppb = BLOCK // page

    @pl.when(c == 0)
    def _():
        qs = q_ref[0] * (HEAD_DIM ** -0.5)
        qrep = jnp.concatenate([qs] * n_heads, axis=0)
        row_head = lax.div(lax.broadcasted_iota(jnp.int32, (rows, d_att), 0), jnp.int32(ts))
        lane_head = lax.div(lax.broadcasted_iota(jnp.int32, (rows, d_att), 1), jnp.int32(HEAD_DIM))
        qbd_scr[...] = jnp.where(row_head == lane_head, qrep, 0.0).astype(BF16)

    qbd = qbd_scr[...]
    slope = _head_slopes((rows, 1), n_heads, ts)
    blk_lane = lax.broadcasted_iota(jnp.int32, (rows, nbs), 1)

    def block_of(page_refs, blk):
        return jnp.concatenate([page_refs[blk * ppb + i][0, 0].reshape(d_att, page) for i in range(ppb)], axis=1)

    def own_scores():
        pad = jnp.zeros((16 - ts, d_att), F32)
        kn = jnp.concatenate([kn_ref[0], pad], axis=0).astype(BF16)
        u_idx = lax.broadcasted_iota(jnp.int32, (rows, 16), 1)
        q_idx = lax.rem(lax.broadcasted_iota(jnp.int32, (rows, 16), 0), jnp.int32(ts))
        so = _dot_nt(qbd, kn) + slope * u_idx.astype(F32)
        return jnp.where(u_idx <= q_idx, so, MASKED)

    @pl.when(c < n_k)
    def _():
        @pl.when(c == 0)
        def _():
            kmean_scr[...] = jnp.zeros(kmean_scr.shape, F32)

        key_bias = slope * lax.broadcasted_iota(jnp.int32, (rows, BLOCK), 1).astype(F32)
        km_lane = lax.broadcasted_iota(jnp.int32, kmean_scr.shape, 1)
        for blk in range(bpc):
            n = c * bpc + blk
            kt = block_of(kp, blk)
            kmean = jnp.sum(kt, axis=1, keepdims=True) * (1.0 / BLOCK)
            kmean_scr[...] = jnp.where(km_lane == n, kmean, kmean_scr[...])
            s_scr[n] = _dot(qbd, kt.astype(BF16)) + key_bias

    @pl.when(c == n_k - 1)
    def _():
        gate = _dot(qbd, kmean_scr[...].astype(BF16))[:, :nbs]
        rank = jnp.zeros((rows, nbs), jnp.int32)
        for m in range(nbs):
            colm = gate[:, m:m + 1]
            rank = rank + jnp.where(colm > gate, 1, jnp.where(colm == gate, jnp.where(blk_lane > m, 1, 0), 0))
        sel = rank < TOPK
        blk_bias = slope * ((blk_lane - nbs) * BLOCK).astype(F32)
        blk_max = jnp.full((rows, nbs), MASKED, F32)
        for n in range(nbs):
            blk_max = jnp.where(blk_lane == n, jnp.max(s_scr[n], axis=1, keepdims=True), blk_max)
        e = jnp.where(sel, blk_max + blk_bias, MASKED)
        mx = jnp.maximum(jnp.max(e, axis=1, keepdims=True), jnp.max(own_scores(), axis=1, keepdims=True))
        mx_scr[...] = mx
        shift_scr[...] = jnp.where(sel, blk_bias - mx, MASKED)
        l_scr[...] = jnp.zeros(l_scr.shape, F32)
        acc_scr[...] = jnp.zeros(acc_scr.shape, F32)

    @pl.when(c >= n_k)
    def _():
        shift = shift_scr[...]
        zero_rows = jnp.zeros((LANES - rows, BLOCK), F32)
        for blk in range(bpc):
            n = (c - n_k) * bpc + blk
            sh = jnp.sum(jnp.where(blk_lane == n, shift, 0.0), axis=1, keepdims=True)
            p = jnp.exp(s_scr[n] + sh)
            l_scr[...] += jnp.sum(p, axis=1, keepdims=True)
            pT = jnp.concatenate([p, zero_rows], axis=0).T.astype(BF16)
            acc_scr[...] += _dot(block_of(vp, blk).astype(BF16), pT)

    @pl.when(c == pl.num_programs(1) - 1)
    def _():
        po = jnp.exp(own_scores() - mx_scr[...])
        pad = jnp.zeros((16 - ts, d_att), F32)
        vn = jnp.concatenate([vn_ref[0], pad], axis=0).astype(BF16)
        num = acc_scr[...].T[:rows] + _dot(po.astype(BF16), vn)
        res = num / (l_scr[...] + jnp.sum(po, axis=1, keepdims=True))
        o_ref[0] = jnp.concatenate([res[h * ts:(h + 1) * ts, h * HEAD_DIM:(h + 1) * HEAD_DIM]
                                    for h in range(n_heads)], axis=-1)


def _moba_sample(q, k_new, v_new, page_table, cache_k, cache_v, layer):
    bs, ts, d_att = q.shape
    n_pages = page_table.shape[1]
    n_heads, _, page = cache_k.shape[2:]
    assert BLOCK % page == 0 and (n_pages * page) % BLOCK == 0 and ts <= 8
    nbs = n_pages * page // BLOCK
    rows = n_heads * ts
    assert nbs <= LANES and rows <= LANES
    ppb = BLOCK // page
    pages = next(p for p in (16, 8, 4, 2, 1) if n_pages % p == 0 and p % ppb == 0)
    n_k = n_pages // pages
    pt = page_table.reshape(-1).astype(jnp.int32)
    seq = pl.BlockSpec((1, ts, d_att), lambda b, c, pt: (b, 0, 0))

    def k_page(p):
        return pl.BlockSpec((1, 1, n_heads, HEAD_DIM, page),
                            lambda b, c, pt: (layer, pt[b * n_pages + jnp.minimum(c, n_k - 1) * pages + p], 0, 0, 0))

    def v_page(p):
        return pl.BlockSpec((1, 1, n_heads, HEAD_DIM, page),
                            lambda b, c, pt: (layer, pt[b * n_pages + jnp.maximum(c - n_k, 0) * pages + p], 0, 0, 0))

    grid_spec = pltpu.PrefetchScalarGridSpec(
        num_scalar_prefetch=1,
        grid=(bs, 2 * n_k),
        in_specs=[seq, seq, seq] + [k_page(p) for p in range(pages)] + [v_page(p) for p in range(pages)],
        out_specs=seq,
        scratch_shapes=[pltpu.VMEM((d_att, LANES), F32),
                        pltpu.VMEM((nbs, rows, BLOCK), F32),
                        pltpu.VMEM((rows, nbs), F32),
                        pltpu.VMEM((rows, 1), F32),
                        pltpu.VMEM((rows, 1), F32),
                        pltpu.VMEM((d_att, LANES), F32),
                        pltpu.VMEM((rows, d_att), BF16)],
    )
    return pl.pallas_call(
        functools.partial(_moba_sample_kernel, pages=pages, n_heads=n_heads, ts=ts, nbs=nbs),
        grid_spec=grid_spec,
        out_shape=jax.ShapeDtypeStruct((bs, ts, d_att), F32),
        compiler_params=_params(2),
        name="moba_sample",
    )(pt, q, k_new, v_new, *([cache_k] * pages), *([cache_v] * pages))


def kernel(x_prompt, x_sample, cache_k, cache_v, state_conv, page_table, p_prompt, p_sample,
           ffn1_w_gate, ffn1_w_up, ffn1_w_down, ln1_g, ln1_b, w_in, b_in, conv_w, conv_b,
           conv_norm_g, conv_norm_b, w_out, ln2_g, ln2_b, ffn2_w_gate, ffn2_w_up, ffn2_w_down,
           ln3_g, ln3_b, ple_w_proj, ple_w_gate):
    depth = ffn1_w_gate.shape[0]
    alpha = (2.0 * depth) ** 0.25
    bsz, t_len, d = x_prompt.shape
    bs, ts, _ = x_sample.shape
    n_heads = cache_k.shape[3]
    assert cache_k.shape[4] == HEAD_DIM and conv_w.shape[1] == CONV_WIDTH
    assert n_heads == 8, "ALiBi slopes must be powers of two for the exact bf16 bias columns"
    assert t_len % BLOCK == 0
    d_att = n_heads * HEAD_DIM
    d_conv = conv_w.shape[2]
    n_state = CONV_WIDTH - 1
    row = lambda a: a.reshape(1, -1)

    cache_kt = cache_k.transpose(0, 1, 3, 4, 2)
    cache_vt = cache_v.transpose(0, 1, 3, 4, 2)
    yp = x_prompt.reshape(bsz * t_len, d)
    ys = x_sample.transpose(1, 0, 2).reshape(ts * bs, d)
    outs = [[] for _ in range(6)]
    for i in range(depth):
        f1 = (ffn1_w_gate[i].astype(BF16), ffn1_w_up[i].astype(BF16), ffn1_w_down[i].astype(BF16),
              row(ln1_g[i]), row(ln1_b[i]))
        mix = (w_in[i].astype(BF16), row(b_in[i]), conv_w[i], row(conv_b[i]),
               row(conv_norm_g[i]), row(conv_norm_b[i]))
        tail = (w_out[i].astype(BF16), row(ln2_g[i]), row(ln2_b[i]),
                ffn2_w_gate[i].astype(BF16), ffn2_w_up[i].astype(BF16), ffn2_w_down[i].astype(BF16),
                row(ln3_g[i]), row(ln3_b[i]), ple_w_gate[i].astype(BF16), ple_w_proj[i].astype(BF16))

        x1 = _ffn_ln(yp, *f1, alpha)
        k, v, conv, utail, kmean, qT, kaug, vT = _inproj_prompt(x1.reshape(bsz, t_len, d), *mix, n_heads)
        nb = t_len // BLOCK
        kmean_h = kmean.reshape(bsz, nb, n_heads, HEAD_DIM).transpose(0, 2, 1, 3)
        att = _moba_prompt(qT, kaug, vT, kmean_h)
        yp = _out_ffn_ple(x1, conv.reshape(-1, d_conv), att.reshape(-1, d_att),
                          p_prompt[i].reshape(bsz * t_len, -1), *tail, alpha)
        outs[0].append(k.reshape(bsz, t_len, n_heads, HEAD_DIM))
        outs[1].append(v.reshape(bsz, t_len, n_heads, HEAD_DIM))
        outs[2].append(utail[:, CONV_HALO - n_state:])

        x1 = _ffn_ln(ys, *f1, alpha)
        state = state_conv[i]
        q, k, v, u, conv = _inproj_sample(x1, *mix, state.transpose(1, 0, 2), n_heads)
        to_bm = lambda a: a.reshape(ts, bs, -1).transpose(1, 0, 2)
        q, k, v, u = to_bm(q), to_bm(k), to_bm(v), to_bm(u)
        att = _moba_sample(q, k, v, page_table, cache_kt, cache_vt, i)
        att = att.transpose(1, 0, 2).reshape(ts * bs, d_att).astype(BF16)
        ys = _out_ffn_ple(x1, conv, att, p_sample[i].transpose(1, 0, 2).reshape(ts * bs, -1), *tail, alpha)
        outs[3].append(k.reshape(bs, ts, n_heads, HEAD_DIM))
        outs[4].append(v.reshape(bs, ts, n_heads, HEAD_DIM))
        outs[5].append(jnp.concatenate([state, u], axis=1)[:, ts:])

    return (yp.reshape(bsz, t_len, d), ys.reshape(ts, bs, d).transpose(1, 0, 2),
            *(jnp.stack(o) for o in outs))
```

```python
import functools

import jax
import jax.numpy as jnp
import numpy as np
from jax import lax
from jax.experimental import pallas as pl
from jax.experimental.pallas import tpu as pltpu

F32 = jnp.float32
BF16 = jnp.bfloat16

LN_EPS = 1e-5
HEAD_DIM = 64
BLOCK = 256
TOPK = 3
CONV_WIDTH = 31
CONV_HALO = 32
CONV_ROWS = 32
SUBLANES = 8
MASKED = -(2.0 ** 100)
SEL_ROWS = 16
BIAS_PARTS = 4
AUG = 128
LOG2E = 1.4426950408889634
CHUNK_PAGES = 16
V_ROWS = 80
LANES = 128
VMEM_LIMIT_BYTES = 56 * 1024 * 1024


def _dot(a, b):
    return jnp.dot(a, b, preferred_element_type=F32)


def _dot_nt(a, b):
    return lax.dot_general(a, b, (((1,), (1,)), ((), ())), preferred_element_type=F32)


def _sigmoid(x):
    return 1.0 / (1.0 + jnp.exp(-x))


def _layer_norm(r, g, b):
    mu = jnp.mean(r, axis=-1, keepdims=True)
    d = r - mu
    var = jnp.mean(d * d, axis=-1, keepdims=True)
    return d * lax.rsqrt(var + LN_EPS) * g + b


def _swiglu(xb, wg_ref, wu_ref, wd_ref, chunks=1):
    d_ff = wg_ref.shape[1]
    step = d_ff // chunks
    assert step * chunks == d_ff and step % LANES == 0
    y = None
    for c in range(chunks):
        cols = slice(c * step, (c + 1) * step)
        g = _dot(xb, wg_ref[:, cols])
        u = _dot(xb, wu_ref[:, cols])
        part = _dot(((g * _sigmoid(g)) * u).astype(BF16), wd_ref[cols, :])
        y = part if y is None else y + part
    return y


def _const_spec(shape):
    nd = len(shape)
    return pl.BlockSpec(shape, lambda *_: (0,) * nd, pipeline_mode=pl.Buffered(1))


def _params(n_axes):
    return pltpu.CompilerParams(dimension_semantics=("arbitrary",) * n_axes,
                                vmem_limit_bytes=VMEM_LIMIT_BYTES)


def _row_tile(n):
    for tm in (512, 256, 128, 64, 32, 16, 8):
        if n % tm == 0:
            return tm
    raise ValueError(f"row count {n} is not a multiple of 8")


def _ffn_ln_kernel(x_ref, wg_ref, wu_ref, wd_ref, g_ref, b_ref, o_ref, *, alpha):
    x = x_ref[...]
    y = _swiglu(x.astype(BF16), wg_ref, wu_ref, wd_ref)
    o_ref[...] = _layer_norm(alpha * x + 0.5 * y, g_ref[...], b_ref[...])


def _ffn_ln(x, wg, wu, wd, g, b, alpha):
    n, d = x.shape
    tm = _row_tile(n)
    return pl.pallas_call(
        functools.partial(_ffn_ln_kernel, alpha=alpha),
        grid=(n // tm,),
        in_specs=[pl.BlockSpec((tm, d), lambda i: (i, 0)),
                  _const_spec(wg.shape), _const_spec(wu.shape), _const_spec(wd.shape),
                  _const_spec(g.shape), _const_spec(b.shape)],
        out_specs=pl.BlockSpec((tm, d), lambda i: (i, 0)),
        out_shape=jax.ShapeDtypeStruct((n, d), F32),
        compiler_params=_params(1),
        name="ffn_ln",
    )(x, wg, wu, wd, g, b)


def _out_ffn_ple_kernel(x1_ref, conv_ref, att_ref, p_ref, wo_ref, g2_ref, b2_ref,
                        wg_ref, wu_ref, wd_ref, g3_ref, b3_ref, pg_ref, pp_ref, o_ref,
                        *, alpha, d_conv):
    m = _dot(conv_ref[...], wo_ref[:d_conv, :]) + _dot(att_ref[...], wo_ref[d_conv:, :])
    x2 = _layer_norm(alpha * x1_ref[...] + m, g2_ref[...], b2_ref[...])
    y = _swiglu(x2.astype(BF16), wg_ref, wu_ref, wd_ref)
    x3 = _layer_norm(alpha * x2 + 0.5 * y, g3_ref[...], b3_ref[...])
    gate = _sigmoid(_dot(x3.astype(BF16), pg_ref[...]))
    o_ref[...] = x3 + gate * _dot(p_ref[...].astype(BF16), pp_ref[...])


def _out_ffn_ple(x1, conv, att, p, wo, g2, b2, wg, wu, wd, g3, b3, pg, pp, alpha):
    n, d = x1.shape
    tm = _row_tile(n)
    rows = lambda a: pl.BlockSpec((tm, a.shape[1]), lambda i: (i, 0))
    consts = (wo, g2, b2, wg, wu, wd, g3, b3, pg, pp)
    return pl.pallas_call(
        functools.partial(_out_ffn_ple_kernel, alpha=alpha, d_conv=conv.shape[1]),
        grid=(n // tm,),
        in_specs=[rows(x1), rows(conv), rows(att), rows(p)] + [_const_spec(c.shape) for c in consts],
        out_specs=rows(x1),
        out_shape=jax.ShapeDtypeStruct((n, d), F32),
        compiler_params=_params(1),
        name="out_ffn_ple",
    )(x1, conv, att, p, *consts)


def _conv_branch_tail(y, cb_ref, cng_ref, cnb_ref):
    c = _layer_norm(y + cb_ref[...], cng_ref[...], cnb_ref[...])
    return c * _sigmoid(c)


def _inproj_prompt_kernel(x1_ref, pos_tab_ref, blk_tab_ref, w_in_ref, b_in_ref, cw_ref, cb_ref, cng_ref, cnb_ref,
                          k_ref, v_ref, conv_ref, utail_ref, kmean_ref, qT_ref, kaug_ref, vT_ref,
                          uext_ref, *, tm, n_heads, d_conv, d_att):
    t = pl.program_id(1)
    bpt = tm // BLOCK
    z = _dot(x1_ref[0].astype(BF16), w_in_ref[...]) + b_in_ref[...]
    a = z[:, :d_conv]
    g = z[:, d_conv:2 * d_conv]
    zq = z[:, 2 * d_conv:2 * d_conv + d_att]
    zk = z[:, 2 * d_conv + d_att:2 * d_conv + 2 * d_att]
    zv = z[:, 2 * d_conv + 2 * d_att:]
    k_ref[0] = zk
    v_ref[0] = zv

    @pl.when(t == 0)
    def _():
        uext_ref[0:CONV_HALO, :] = jnp.zeros((CONV_HALO, d_conv), F32)

    uext_ref[CONV_HALO:CONV_HALO + tm, :] = a * _sigmoid(g)
    off = CONV_HALO - (CONV_WIDTH - 1)
    for r0 in range(0, tm, CONV_ROWS):
        window = uext_ref[r0:r0 + CONV_ROWS + CONV_HALO, :]
        n_win = CONV_ROWS + CONV_HALO
        shifted = [window] + [pltpu.roll(window, n_win - s, 0) for s in range(1, SUBLANES)]
        y = jnp.zeros((CONV_ROWS, d_conv), F32)
        for j in range(CONV_WIDTH):
            s = (off + j) % SUBLANES
            base = off + j - s
            y = y + cw_ref[j:j + 1, :] * shifted[s][base:base + CONV_ROWS, :]
        conv_ref[0, r0:r0 + CONV_ROWS, :] = _conv_branch_tail(y, cb_ref, cng_ref, cnb_ref).astype(BF16)
    tail = uext_ref[tm:tm + CONV_HALO, :]
    utail_ref[0] = tail
    uext_ref[0:CONV_HALO, :] = tail

    @pl.when(t == 0)
    def _():
        kmean_ref[0] = jnp.zeros(kmean_ref.shape[1:], F32)

    km = kmean_ref[0]
    km_row = lax.broadcasted_iota(jnp.int32, km.shape, 0)
    for blk in range(bpt):
        mean = jnp.sum(zk[blk * BLOCK:(blk + 1) * BLOCK], axis=0, keepdims=True) * (1.0 / BLOCK)
        km = jnp.where(km_row == t * bpt + blk, mean, km)
    kmean_ref[0] = km

    zqT = (zq * (HEAD_DIM ** -0.5 * LOG2E)).T
    zvT = zv.T
    lane = lax.broadcasted_iota(jnp.int32, (BLOCK, LANES), 1)
    ones_row = jnp.where(lax.broadcasted_iota(jnp.int32, (V_ROWS - HEAD_DIM, BLOCK), 0) == 0, 1.0, 0.0)
    for blk in range(bpt):
        rows = slice(blk * BLOCK, (blk + 1) * BLOCK)
        for h in range(n_heads):
            hd = slice(h * HEAD_DIM, (h + 1) * HEAD_DIM)
            qT_ref[0, h, blk] = zqT[hd, rows].astype(BF16)
            vT_ref[0, h, blk] = jnp.concatenate([zvT[hd, rows], ones_row], axis=0).astype(BF16)
            pair = zk[rows, (h // 2) * LANES:(h // 2 + 1) * LANES]
            if h % 2:
                pair = pltpu.roll(pair, HEAD_DIM, 1)
            extra = pos_tab_ref[h] + blk_tab_ref[0, h, blk:blk + 1, :]
            kaug_ref[0, h, blk] = jnp.where(lane < HEAD_DIM, pair, extra).astype(BF16)


def _split_bf16(x, parts):
    out = []
    for _ in range(parts):
        piece = x.astype(jnp.bfloat16).astype(np.float64)
        out.append(piece.astype(np.float32))
        x = x - piece
    return out


def _alibi_tables(n_heads, nb, bpt):
    slopes = 2.0 ** (-8.0 * (np.arange(n_heads, dtype=np.float64) + 1.0) / n_heads) * LOG2E
    c0 = HEAD_DIM + SEL_ROWS
    pos_tab = np.zeros((n_heads, BLOCK, AUG), np.float32)
    for k, piece in enumerate(_split_bf16(slopes[:, None] * np.arange(BLOCK, dtype=np.float64)[None, :], BIAS_PARTS)):
        pos_tab[:, :, c0 + k] = piece
    blk_tab = np.zeros((nb, n_heads, AUG), np.float32)
    blk_tab[np.arange(nb), :, HEAD_DIM + np.arange(nb)] = 1.0
    blk_bias = slopes[None, :] * (BLOCK * np.arange(nb, dtype=np.float64))[:, None]
    for k, piece in enumerate(_split_bf16(blk_bias, BIAS_PARTS)):
        blk_tab[:, :, c0 + BIAS_PARTS + k] = piece
    blk_tab = blk_tab.reshape(nb // bpt, bpt, n_heads, AUG).transpose(0, 2, 1, 3)
    return jnp.asarray(pos_tab), jnp.asarray(blk_tab)


def _inproj_prompt(x1, w_in, b_in, cw, cb, cng, cnb, n_heads):
    bsz, t_len, d = x1.shape
    d_conv = cw.shape[1]
    d_att = n_heads * HEAD_DIM
    tm = 512 if t_len % 512 == 0 else BLOCK
    bpt = tm // BLOCK
    nb = t_len // BLOCK
    tiles = lambda w: pl.BlockSpec((1, tm, w), lambda b, t: (b, t, 0))
    per_seq = lambda r, w: pl.BlockSpec((1, r, w), lambda b, t: (b, 0, 0))
    head_blocks = lambda r, w: pl.BlockSpec((1, n_heads, bpt, r, w), lambda b, t: (b, 0, t, 0, 0))
    pos_tab, blk_tab = _alibi_tables(n_heads, nb, bpt)
    consts = (w_in, b_in, cw, cb, cng, cnb)
    return pl.pallas_call(
        functools.partial(_inproj_prompt_kernel, tm=tm, n_heads=n_heads, d_conv=d_conv, d_att=d_att),
        grid=(bsz, t_len // tm),
        in_specs=[tiles(d), _const_spec(pos_tab.shape),
                  pl.BlockSpec((1, n_heads, bpt, AUG), lambda b, t: (t, 0, 0, 0))]
                 + [_const_spec(c.shape) for c in consts],
        out_specs=[tiles(d_att), tiles(d_att), tiles(d_conv), per_seq(CONV_HALO, d_conv), per_seq(nb, d_att),
                   head_blocks(HEAD_DIM, BLOCK), head_blocks(BLOCK, AUG), head_blocks(V_ROWS, BLOCK)],
        out_shape=[jax.ShapeDtypeStruct((bsz, t_len, d_att), F32),
                   jax.ShapeDtypeStruct((bsz, t_len, d_att), F32),
                   jax.ShapeDtypeStruct((bsz, t_len, d_conv), BF16),
                   jax.ShapeDtypeStruct((bsz, CONV_HALO, d_conv), F32),
                   jax.ShapeDtypeStruct((bsz, nb, d_att), F32),
                   jax.ShapeDtypeStruct((bsz, n_heads, nb, HEAD_DIM, BLOCK), BF16),
                   jax.ShapeDtypeStruct((bsz, n_heads, nb, BLOCK, AUG), BF16),
                   jax.ShapeDtypeStruct((bsz, n_heads, nb, V_ROWS, BLOCK), BF16)],
        scratch_shapes=[pltpu.VMEM((CONV_HALO + tm, d_conv), F32)],
        compiler_params=_params(2),
        name="inproj_prompt",
    )(x1, pos_tab, blk_tab, *consts)


def _topk_select(gate, idx, n_valid):
    n = gate.shape[0]
    valid = idx < n_valid
    gm = jnp.where(valid, gate, -jnp.inf)
    rank = jnp.zeros(gate.shape, jnp.int32)
    for m in range(n):
        row = gm[m:m + 1, :]
        rank = rank + jnp.where(row > gm, 1, jnp.where(row == gm, jnp.where(idx > m, 1, 0), 0))
    return (rank < TOPK) & valid


def _moba_prompt_kernel(qT_ref, kaug_ref, vT_ref, kmean_ref, o_ref, qaug_scr, s_scr, m_scr, acc_scr,
                        *, nb, n_heads):
    i = pl.program_id(1)
    blk_idx = lax.broadcasted_iota(jnp.int32, (nb, BLOCK), 0)
    causal = (lax.broadcasted_iota(jnp.int32, (BLOCK, BLOCK), 0)
              <= lax.broadcasted_iota(jnp.int32, (BLOCK, BLOCK), 1))
    aug_row = lax.broadcasted_iota(jnp.int32, (AUG - HEAD_DIM - SEL_ROWS, BLOCK), 0)
    bias_rows = jnp.where(aug_row < 2 * BIAS_PARTS, 1.0, 0.0).astype(BF16)

    def absorb(h, st, j):
        m = m_scr[h]
        m_new = jnp.maximum(m, jnp.max(st, axis=0, keepdims=True))
        p = jnp.exp2(st - m_new).astype(BF16)
        acc_scr[h] = acc_scr[h] * jnp.exp2(m - m_new) + _dot(vT_ref[0, h, j], p)
        m_scr[h] = m_new

    gates = [_dot(kmean_ref[0, h].astype(BF16), qT_ref[0, h, 0]) for h in range(n_heads)]
    for h in range(n_heads):
        keep = _topk_select(gates[h], blk_idx, i) | (blk_idx == i)
        selbias = jnp.where(keep, 0.0, MASKED)
        if nb < SEL_ROWS:
            selbias = jnp.concatenate([selbias, jnp.zeros((SEL_ROWS - nb, BLOCK), F32)], axis=0)
        qaug_scr[h] = jnp.concatenate([qT_ref[0, h, 0], selbias.astype(BF16), bias_rows], axis=0)
        m_scr[h] = jnp.full((1, BLOCK), MASKED, F32)
        acc_scr[h] = jnp.zeros((V_ROWS, BLOCK), F32)
    for h in range(n_heads):
        s_scr[0, h] = _dot(kaug_ref[0, h, 0], qaug_scr[h])

    def past_block(j, cur):
        for h in range(n_heads):
            s_scr[1 - cur, h] = _dot(kaug_ref[0, h, j + 1], qaug_scr[h])
        for h in range(n_heads):
            absorb(h, s_scr[cur, h], j)

    def past_pair(t, carry):
        past_block(2 * t, 0)
        past_block(2 * t + 1, 1)
        return carry

    def own_block(cur):
        for h in range(n_heads):
            absorb(h, jnp.where(causal, s_scr[cur, h], MASKED), i)
        outs = [acc_scr[h, :HEAD_DIM, :] / acc_scr[h, HEAD_DIM:HEAD_DIM + 1, :] for h in range(n_heads)]
        o_ref[0] = jnp.concatenate(outs, axis=0).T.astype(BF16)

    lax.fori_loop(0, i // 2, past_pair, 0)

    @pl.when(i % 2 == 0)
    def _():
        own_block(0)

    @pl.when(i % 2 == 1)
    def _():
        past_block(i - 1, 0)
        own_block(1)


def _moba_prompt(qT, kaug, vT, kmean):
    bsz, n_heads, nb = qT.shape[:3]
    assert nb <= SEL_ROWS, "prompt sequences longer than SEL_ROWS key blocks are not supported"
    seq = lambda r, w: pl.BlockSpec((1, n_heads, nb, r, w), lambda b, i: (b, 0, 0, 0, 0))
    return pl.pallas_call(
        functools.partial(_moba_prompt_kernel, nb=nb, n_heads=n_heads),
        grid=(bsz, nb),
        in_specs=[pl.BlockSpec((1, n_heads, 1, HEAD_DIM, BLOCK), lambda b, i: (b, 0, i, 0, 0)),
                  seq(BLOCK, AUG), seq(V_ROWS, BLOCK),
                  pl.BlockSpec((1, n_heads, nb, HEAD_DIM), lambda b, i: (b, 0, 0, 0))],
        out_specs=pl.BlockSpec((1, BLOCK, n_heads * HEAD_DIM), lambda b, i: (b, i, 0)),
        out_shape=jax.ShapeDtypeStruct((bsz, nb * BLOCK, n_heads * HEAD_DIM), BF16),
        scratch_shapes=[pltpu.VMEM((n_heads, AUG, BLOCK), BF16),
                        pltpu.VMEM((2, n_heads, BLOCK, BLOCK), F32),
                        pltpu.VMEM((n_heads, 1, BLOCK), F32),
                        pltpu.VMEM((n_heads, V_ROWS, BLOCK), F32)],
        compiler_params=_params(2),
        name="moba_prompt",
    )(qT, kaug, vT, kmean)


def _inproj_sample_kernel(x1_ref, w_in_ref, b_in_ref, cw_ref, cb_ref, cng_ref, cnb_ref, state_ref,
                          q_ref, k_ref, v_ref, u_ref, conv_ref, *, bs, ts, d_conv, d_att):
    z = _dot(x1_ref[...].astype(BF16), w_in_ref[...]) + b_in_ref[...]
    a = z[:, :d_conv]
    g = z[:, d_conv:2 * d_conv]
    q_ref[...] = z[:, 2 * d_conv:2 * d_conv + d_att]
    k_ref[...] = z[:, 2 * d_conv + d_att:2 * d_conv + 2 * d_att]
    v_ref[...] = z[:, 2 * d_conv + 2 * d_att:]
    u_ref[...] = a * _sigmoid(g)
    n_state = CONV_WIDTH - 1
    for t in range(ts):
        y = jnp.zeros((bs, d_conv), F32)
        for j in range(CONV_WIDTH):
            tau = t + j
            if tau < n_state:
                slab = state_ref[tau]
            else:
                slab = u_ref[(tau - n_state) * bs:(tau - n_state + 1) * bs, :]
            y = y + cw_ref[j:j + 1, :] * slab
        conv_ref[t * bs:(t + 1) * bs, :] = _conv_branch_tail(y, cb_ref, cng_ref, cnb_ref).astype(BF16)


def _inproj_sample(x1, w_in, b_in, cw, cb, cng, cnb, state_tm, n_heads):
    n, d = x1.shape
    bs = state_tm.shape[1]
    ts = n // bs
    d_conv = cw.shape[1]
    d_att = n_heads * HEAD_DIM
    ins = (x1, w_in, b_in, cw, cb, cng, cnb, state_tm)
    out = lambda w, dt: jax.ShapeDtypeStruct((n, w), dt)
    outs = [out(d_att, F32), out(d_att, F32), out(d_att, F32), out(d_conv, F32), out(d_conv, BF16)]
    return pl.pallas_call(
        functools.partial(_inproj_sample_kernel, bs=bs, ts=ts, d_conv=d_conv, d_att=d_att),
        grid=(1,),
        in_specs=[_const_spec(a.shape) for a in ins],
        out_specs=[pl.BlockSpec(o.shape, lambda i: (0, 0)) for o in outs],
        out_shape=outs,
        compiler_params=_params(1),
        name="inproj_sample",
    )(*ins)


def _head_slopes(shape, n_heads, rows_per_head):
    head = lax.div(lax.broadcasted_iota(jnp.int32, shape, 0), jnp.int32(rows_per_head))
    out = jnp.zeros(shape, F32)
    for h in range(n_heads):
        out = jnp.where(head == h, 2.0 ** (-8.0 * (h + 1) / n_heads), out)
    return out


def _ffn_ln_moba_kernel(pt_ref, x_ref, wg_ref, wu_ref, wd_ref, g_ref, b_ref, q_ref, kn_ref, vn_ref, ck_hbm, cv_hbm,
                        o_ref, att_ref,
                        page_buf, sems, kmean_scr, s_scr, shift_scr, mx_scr, l_scr, acc_scr, qbd_scr,
                        *, alpha, layer, n_heads, ts, n_pages):
    step = pl.program_id(0)
    seq = step // 2
    page = page_buf.shape[-1]
    d_att = n_heads * HEAD_DIM
    rows = n_heads * ts
    ppb = BLOCK // page
    nbs = n_pages // ppb
    n_chunks = n_pages // CHUNK_PAGES
    bpc = CHUNK_PAGES // ppb

    def page_copy(src_hbm, page_idx, slot):
        return pltpu.make_async_copy(src_hbm.at[layer, page_idx], page_buf.at[slot], sems.at[slot // CHUNK_PAGES])

    def start_chunk(src_hbm, seq_idx, ci):
        for slot in range(ci * CHUNK_PAGES, (ci + 1) * CHUNK_PAGES):
            page_copy(src_hbm, pt_ref[seq_idx * n_pages + slot], slot).start()

    def wait_chunk(src_hbm, ci):
        for slot in range(ci * CHUNK_PAGES, (ci + 1) * CHUNK_PAGES):
            page_copy(src_hbm, 0, slot).wait()

    def block_of(n):
        return jnp.concatenate([page_buf[n * ppb + i].reshape(d_att, page) for i in range(ppb)], axis=1)

    slope = _head_slopes((rows, 1), n_heads, ts)
    blk_lane = lax.broadcasted_iota(jnp.int32, (rows, nbs), 1)

    def own_scores(qbd):
        pad = jnp.zeros((16 - ts, d_att), F32)
        kn = jnp.concatenate([kn_ref[0], pad], axis=0).astype(BF16)
        u_idx = lax.broadcasted_iota(jnp.int32, (rows, 16), 1)
        q_idx = lax.rem(lax.broadcasted_iota(jnp.int32, (rows, 16), 0), jnp.int32(ts))
        so = _dot_nt(qbd, kn) + slope * u_idx.astype(F32)
        return jnp.where(u_idx <= q_idx, so, MASKED)

    @pl.when(step % 2 == 0)
    def _():
        @pl.when(step == 0)
        def _():
            for ci in range(n_chunks):
                start_chunk(ck_hbm, seq, ci)

        qs = q_ref[0] * (HEAD_DIM ** -0.5)
        qrep = jnp.concatenate([qs] * n_heads, axis=0)
        row_head = lax.div(lax.broadcasted_iota(jnp.int32, (rows, d_att), 0), jnp.int32(ts))
        lane_head = lax.div(lax.broadcasted_iota(jnp.int32, (rows, d_att), 1), jnp.int32(HEAD_DIM))
        qbd = jnp.where(row_head == lane_head, qrep, 0.0).astype(BF16)
        qbd_scr[...] = qbd
        kmean_scr[...] = jnp.zeros(kmean_scr.shape, F32)
        key_bias = slope * lax.broadcasted_iota(jnp.int32, (rows, BLOCK), 1).astype(F32)
        km_lane = lax.broadcasted_iota(jnp.int32, kmean_scr.shape, 1)
        for ci in range(n_chunks):
            wait_chunk(ck_hbm, ci)
            for n in range(ci * bpc, (ci + 1) * bpc):
                kt = block_of(n)
                kmean = jnp.sum(kt, axis=1, keepdims=True) * (1.0 / BLOCK)
                kmean_scr[...] = jnp.where(km_lane == n, kmean, kmean_scr[...])
                s_scr[n] = _dot(qbd, kt.astype(BF16)) + key_bias
            start_chunk(cv_hbm, seq, ci)

        gate = _dot(qbd, kmean_scr[...].astype(BF16))[:, :nbs]
        rank = jnp.zeros((rows, nbs), jnp.int32)
        for m in range(nbs):
            colm = gate[:, m:m + 1]
            rank = rank + jnp.where(colm > gate, 1, jnp.where(colm == gate, jnp.where(blk_lane > m, 1, 0), 0))
        sel = rank < TOPK
        blk_bias = slope * ((blk_lane - nbs) * BLOCK).astype(F32)
        blk_max = jnp.full((rows, nbs), MASKED, F32)
        for n in range(nbs):
            blk_max = jnp.where(blk_lane == n, jnp.max(s_scr[n], axis=1, keepdims=True), blk_max)
        e = jnp.where(sel, blk_max + blk_bias, MASKED)
        mx = jnp.maximum(jnp.max(e, axis=1, keepdims=True), jnp.max(own_scores(qbd), axis=1, keepdims=True))
        mx_scr[...] = mx
        shift_scr[...] = jnp.where(sel, blk_bias - mx, MASKED)
        l_scr[...] = jnp.zeros(l_scr.shape, F32)
        acc_scr[...] = jnp.zeros(acc_scr.shape, F32)

    @pl.when(step % 2 == 1)
    def _():
        shift = shift_scr[...]
        zero_rows = jnp.zeros((LANES - rows, BLOCK), F32)
        for ci in range(n_chunks):
            wait_chunk(cv_hbm, ci)
            for n in range(ci * bpc, (ci + 1) * bpc):
                p = jnp.exp(s_scr[n] + shift[:, n:n + 1])
                l_scr[...] += jnp.sum(p, axis=1, keepdims=True)
                pT = jnp.concatenate([p, zero_rows], axis=0).T.astype(BF16)
                acc_scr[...] += _dot(block_of(n).astype(BF16), pT)

            @pl.when(step + 1 < pl.num_programs(0))
            def _():
                start_chunk(ck_hbm, seq + 1, ci)

        po = jnp.exp(own_scores(qbd_scr[...]) - mx_scr[...])
        pad = jnp.zeros((16 - ts, d_att), F32)
        vn = jnp.concatenate([vn_ref[0], pad], axis=0).astype(BF16)
        num = acc_scr[...].T[:rows] + _dot(po.astype(BF16), vn)
        res = num / (l_scr[...] + jnp.sum(po, axis=1, keepdims=True))
        att_ref[0] = jnp.concatenate([res[h * ts:(h + 1) * ts, h * HEAD_DIM:(h + 1) * HEAD_DIM]
                                      for h in range(n_heads)], axis=-1)

    x = x_ref[...]
    y = _swiglu(x.astype(BF16), wg_ref, wu_ref, wd_ref)
    o_ref[...] = _layer_norm(alpha * x + 0.5 * y, g_ref[...], b_ref[...])


def _ffn_ln_moba(x, wg, wu, wd, g, b, alpha, q, k_new, v_new, page_table, cache_k, cache_v, layer):
    n, d = x.shape
    bs, ts, d_att = q.shape
    n_pages = page_table.shape[1]
    n_heads, _, page = cache_k.shape[2:]
    steps = 2 * bs
    tm = n // steps
    assert tm * steps == n and tm % 8 == 0, "prompt rows must split evenly over two grid steps per sample sequence"
    assert BLOCK % page == 0 and (n_pages * page) % BLOCK == 0 and ts <= 8
    assert n_pages % CHUNK_PAGES == 0 and CHUNK_PAGES % (BLOCK // page) == 0
    nbs = n_pages * page // BLOCK
    rows = n_heads * ts
    assert nbs <= LANES and rows <= LANES
    pt = page_table.reshape(-1).astype(jnp.int32)
    row_tile = pl.BlockSpec((tm, d), lambda s, pt: (s, 0))
    seq = pl.BlockSpec((1, ts, d_att), lambda s, pt: (s // 2, 0, 0))
    hbm = pl.BlockSpec(memory_space=pl.ANY)
    consts = (wg, wu, wd, g, b)
    grid_spec = pltpu.PrefetchScalarGridSpec(
        num_scalar_prefetch=1,
        grid=(steps,),
        in_specs=[row_tile] + [_const_spec(c.shape) for c in consts] + [seq, seq, seq, hbm, hbm],
        out_specs=[row_tile, seq],
        scratch_shapes=[pltpu.VMEM((n_pages, n_heads, HEAD_DIM, page), F32),
                        pltpu.SemaphoreType.DMA((n_pages // CHUNK_PAGES,)),
                        pltpu.VMEM((d_att, LANES), F32),
                        pltpu.VMEM((nbs, rows, BLOCK), F32),
                        pltpu.VMEM((rows, nbs), F32),
                        pltpu.VMEM((rows, 1), F32),
                        pltpu.VMEM((rows, 1), F32),
                        pltpu.VMEM((d_att, LANES), F32),
                        pltpu.VMEM((rows, d_att), BF16)],
    )
    return pl.pallas_call(
        functools.partial(_ffn_ln_moba_kernel, alpha=alpha, layer=layer, n_heads=n_heads, ts=ts, n_pages=n_pages),
        grid_spec=grid_spec,
        out_shape=[jax.ShapeDtypeStruct((n, d), F32), jax.ShapeDtypeStruct((bs, ts, d_att), F32)],
        compiler_params=_params(1),
        name="ffn_ln_moba",
    )(pt, x, *consts, q, k_new, v_new, cache_k, cache_v)


def kernel(x_prompt, x_sample, cache_k, cache_v, state_conv, page_table, p_prompt, p_sample,
           ffn1_w_gate, ffn1_w_up, ffn1_w_down, ln1_g, ln1_b, w_in, b_in, conv_w, conv_b,
           conv_norm_g, conv_norm_b, w_out, ln2_g, ln2_b, ffn2_w_gate, ffn2_w_up, ffn2_w_down,
           ln3_g, ln3_b, ple_w_proj, ple_w_gate):
    depth = ffn1_w_gate.shape[0]
    alpha = (2.0 * depth) ** 0.25
    bsz, t_len, d = x_prompt.shape
    bs, ts, _ = x_sample.shape
    n_heads = cache_k.shape[3]
    assert cache_k.shape[4] == HEAD_DIM and conv_w.shape[1] == CONV_WIDTH
    assert n_heads == 8, "ALiBi slopes must be powers of two for the exact bf16 bias columns"
    assert t_len % BLOCK == 0
    d_att = n_heads * HEAD_DIM
    d_conv = conv_w.shape[2]
    n_state = CONV_WIDTH - 1
    row = lambda a: a.reshape(1, -1)

    cache_kt = cache_k.transpose(0, 1, 3, 4, 2)
    cache_vt = cache_v.transpose(0, 1, 3, 4, 2)
    yp = x_prompt.reshape(bsz * t_len, d)
    ys = x_sample.transpose(1, 0, 2).reshape(ts * bs, d)
    outs = [[] for _ in range(6)]
    for i in range(depth):
        f1 = (ffn1_w_gate[i].astype(BF16), ffn1_w_up[i].astype(BF16), ffn1_w_down[i].astype(BF16),
              row(ln1_g[i]), row(ln1_b[i]))
        mix = (w_in[i].astype(BF16), row(b_in[i]), conv_w[i], row(conv_b[i]),
               row(conv_norm_g[i]), row(conv_norm_b[i]))
        tail = (w_out[i].astype(BF16), row(ln2_g[i]), row(ln2_b[i]),
                ffn2_w_gate[i].astype(BF16), ffn2_w_up[i].astype(BF16), ffn2_w_down[i].astype(BF16),
                row(ln3_g[i]), row(ln3_b[i]), ple_w_gate[i].astype(BF16), ple_w_proj[i].astype(BF16))

        x1s = _ffn_ln(ys, *f1, alpha)
        state = state_conv[i]
        q, ks, vs, u, convs = _inproj_sample(x1s, *mix, state.transpose(1, 0, 2), n_heads)
        to_bm = lambda a: a.reshape(ts, bs, -1).transpose(1, 0, 2)
        q, ks, vs, u = to_bm(q), to_bm(ks), to_bm(vs), to_bm(u)

        x1, atts = _ffn_ln_moba(yp, *f1, alpha, q, ks, vs, page_table, cache_kt, cache_vt, i)
        k, v, conv, utail, kmean, qT, kaug, vT = _inproj_prompt(x1.reshape(bsz, t_len, d), *mix, n_heads)
        nb = t_len // BLOCK
        kmean_h = kmean.reshape(bsz, nb, n_heads, HEAD_DIM).transpose(0, 2, 1, 3)
        att = _moba_prompt(qT, kaug, vT, kmean_h)
        yp = _out_ffn_ple(x1, conv.reshape(-1, d_conv), att.reshape(-1, d_att),
                          p_prompt[i].reshape(bsz * t_len, -1), *tail, alpha)
        outs[0].append(k.reshape(bsz, t_len, n_heads, HEAD_DIM))
        outs[1].append(v.reshape(bsz, t_len, n_heads, HEAD_DIM))
        outs[2].append(utail[:, CONV_HALO - n_state:])

        atts = atts.transpose(1, 0, 2).reshape(ts * bs, d_att).astype(BF16)
        ys = _out_ffn_ple(x1s, convs, atts, p_sample[i].transpose(1, 0, 2).reshape(ts * bs, -1), *tail, alpha)
        outs[3].append(ks.reshape(bs, ts, n_heads, HEAD_DIM))
        outs[4].append(vs.reshape(bs, ts, n_heads, HEAD_DIM))
        outs[5].append(jnp.concatenate([state, u], axis=1)[:, ts:])

    return (yp.reshape(bsz, t_len, d), ys.reshape(ts, bs, d).transpose(1, 0, 2),
            *(jnp.stack(o) for o in outs))
```

```python
import functools

import jax
import jax.numpy as jnp
import numpy as np
from jax import lax
from jax.experimental import pallas as pl
from jax.experimental.pallas import tpu as pltpu

F32 = jnp.float32
BF16 = jnp.bfloat16

LN_EPS = 1e-5
HEAD_DIM = 64
BLOCK = 256
TOPK = 3
CONV_WIDTH = 31
CONV_HALO = 32
CONV_ROWS = 32
SUBLANES = 8
MASKED = -(2.0 ** 100)
SEL_ROWS = 16
BIAS_PARTS = 4
AUG = 128
LOG2E = 1.4426950408889634
CHUNK_PAGES = 16
V_ROWS = 80
LANES = 128
MXU_COLS = 256
VMEM_LIMIT_BYTES = 56 * 1024 * 1024


def _dot(a, b):
    return jnp.dot(a, b, preferred_element_type=F32)


def _dot_nt(a, b):
    return lax.dot_general(a, b, (((1,), (1,)), ((), ())), preferred_element_type=F32)


def _sigmoid(x):
    return 1.0 / (1.0 + jnp.exp(-x))


def _layer_norm(r, g, b):
    mu = jnp.mean(r, axis=-1, keepdims=True)
    d = r - mu
    var = jnp.mean(d * d, axis=-1, keepdims=True)
    return d * lax.rsqrt(var + LN_EPS) * g + b


def _swiglu(xb, wg_ref, wu_ref, wd_ref, chunks=1):
    d_ff = wg_ref.shape[1]
    step = d_ff // chunks
    assert step * chunks == d_ff and step % LANES == 0
    y = None
    for c in range(chunks):
        cols = slice(c * step, (c + 1) * step)
        g = _dot(xb, wg_ref[:, cols])
        u = _dot(xb, wu_ref[:, cols])
        part = _dot(((g * _sigmoid(g)) * u).astype(BF16), wd_ref[cols, :])
        y = part if y is None else y + part
    return y


def _const_spec(shape):
    nd = len(shape)
    return pl.BlockSpec(shape, lambda *_: (0,) * nd, pipeline_mode=pl.Buffered(1))


def _params(n_axes):
    return pltpu.CompilerParams(dimension_semantics=("arbitrary",) * n_axes,
                                vmem_limit_bytes=VMEM_LIMIT_BYTES)


def _row_tile(n):
    for tm in (512, 256, 128, 64, 32, 16, 8):
        if n % tm == 0:
            return tm
    raise ValueError(f"row count {n} is not a multiple of 8")


def _ffn_ln_kernel(x_ref, wg_ref, wu_ref, wd_ref, g_ref, b_ref, o_ref, *, alpha):
    x = x_ref[...]
    y = _swiglu(x.astype(BF16), wg_ref, wu_ref, wd_ref)
    o_ref[...] = _layer_norm(alpha * x + 0.5 * y, g_ref[...], b_ref[...])


def _ffn_ln(x, wg, wu, wd, g, b, alpha):
    n, d = x.shape
    tm = _row_tile(n)
    return pl.pallas_call(
        functools.partial(_ffn_ln_kernel, alpha=alpha),
        grid=(n // tm,),
        in_specs=[pl.BlockSpec((tm, d), lambda i: (i, 0)),
                  _const_spec(wg.shape), _const_spec(wu.shape), _const_spec(wd.shape),
                  _const_spec(g.shape), _const_spec(b.shape)],
        out_specs=pl.BlockSpec((tm, d), lambda i: (i, 0)),
        out_shape=jax.ShapeDtypeStruct((n, d), F32),
        compiler_params=_params(1),
        name="ffn_ln",
    )(x, wg, wu, wd, g, b)


def _out_ffn_ple_kernel(x1_ref, conv_ref, att_ref, p_ref, wo_ref, g2_ref, b2_ref,
                        wg_ref, wu_ref, wd_ref, g3_ref, b3_ref, pg_ref, pp_ref, o_ref,
                        *, alpha, d_conv):
    m = _dot(conv_ref[...], wo_ref[:d_conv, :]) + _dot(att_ref[...], wo_ref[d_conv:, :])
    x2 = _layer_norm(alpha * x1_ref[...] + m, g2_ref[...], b2_ref[...])
    y = _swiglu(x2.astype(BF16), wg_ref, wu_ref, wd_ref)
    x3 = _layer_norm(alpha * x2 + 0.5 * y, g3_ref[...], b3_ref[...])
    gate = _sigmoid(_dot(x3.astype(BF16), pg_ref[...]))
    o_ref[...] = x3 + gate * _dot(p_ref[...].astype(BF16), pp_ref[...])


def _out_ffn_ple(x1, conv, att, p, wo, g2, b2, wg, wu, wd, g3, b3, pg, pp, alpha):
    n, d = x1.shape
    tm = _row_tile(n)
    rows = lambda a: pl.BlockSpec((tm, a.shape[1]), lambda i: (i, 0))
    consts = (wo, g2, b2, wg, wu, wd, g3, b3, pg, pp)
    return pl.pallas_call(
        functools.partial(_out_ffn_ple_kernel, alpha=alpha, d_conv=conv.shape[1]),
        grid=(n // tm,),
        in_specs=[rows(x1), rows(conv), rows(att), rows(p)] + [_const_spec(c.shape) for c in consts],
        out_specs=rows(x1),
        out_shape=jax.ShapeDtypeStruct((n, d), F32),
        compiler_params=_params(1),
        name="out_ffn_ple",
    )(x1, conv, att, p, *consts)


def _conv_branch_tail(y, cb_ref, cng_ref, cnb_ref):
    c = _layer_norm(y + cb_ref[...], cng_ref[...], cnb_ref[...])
    return c * _sigmoid(c)


def _inproj_prompt_kernel(x1_ref, pos_tab_ref, blk_tab_ref, w_in_ref, b_in_ref, cw_ref, cb_ref, cng_ref, cnb_ref,
                          k_ref, v_ref, conv_ref, utail_ref, kmean_ref, qT_ref, kaug_ref, vT_ref,
                          uext_ref, *, tm, n_heads, d_conv, d_att):
    t = pl.program_id(1)
    bpt = tm // BLOCK
    z = _dot(x1_ref[0].astype(BF16), w_in_ref[...]) + b_in_ref[...]
    a = z[:, :d_conv]
    g = z[:, d_conv:2 * d_conv]
    zq = z[:, 2 * d_conv:2 * d_conv + d_att]
    zk = z[:, 2 * d_conv + d_att:2 * d_conv + 2 * d_att]
    zv = z[:, 2 * d_conv + 2 * d_att:]
    k_ref[0] = zk
    v_ref[0] = zv

    @pl.when(t == 0)
    def _():
        uext_ref[0:CONV_HALO, :] = jnp.zeros((CONV_HALO, d_conv), F32)

    uext_ref[CONV_HALO:CONV_HALO + tm, :] = a * _sigmoid(g)
    off = CONV_HALO - (CONV_WIDTH - 1)
    for r0 in range(0, tm, CONV_ROWS):
        window = uext_ref[r0:r0 + CONV_ROWS + CONV_HALO, :]
        n_win = CONV_ROWS + CONV_HALO
        shifted = [window] + [pltpu.roll(window, n_win - s, 0) for s in range(1, SUBLANES)]
        y = jnp.zeros((CONV_ROWS, d_conv), F32)
        for j in range(CONV_WIDTH):
            s = (off + j) % SUBLANES
            base = off + j - s
            y = y + cw_ref[j:j + 1, :] * shifted[s][base:base + CONV_ROWS, :]
        conv_ref[0, r0:r0 + CONV_ROWS, :] = _conv_branch_tail(y, cb_ref, cng_ref, cnb_ref).astype(BF16)
    tail = uext_ref[tm:tm + CONV_HALO, :]
    utail_ref[0] = tail
    uext_ref[0:CONV_HALO, :] = tail

    @pl.when(t == 0)
    def _():
        kmean_ref[0] = jnp.zeros(kmean_ref.shape[1:], F32)

    km = kmean_ref[0]
    km_row = lax.broadcasted_iota(jnp.int32, km.shape, 0)
    for blk in range(bpt):
        mean = jnp.sum(zk[blk * BLOCK:(blk + 1) * BLOCK], axis=0, keepdims=True) * (1.0 / BLOCK)
        km = jnp.where(km_row == t * bpt + blk, mean, km)
    kmean_ref[0] = km

    zqT = (zq * (HEAD_DIM ** -0.5 * LOG2E)).T
    zvT = zv.T
    lane = lax.broadcasted_iota(jnp.int32, (BLOCK, LANES), 1)
    ones_row = jnp.where(lax.broadcasted_iota(jnp.int32, (V_ROWS - HEAD_DIM, BLOCK), 0) == 0, 1.0, 0.0)
    for blk in range(bpt):
        rows = slice(blk * BLOCK, (blk + 1) * BLOCK)
        for h in range(n_heads):
            hd = slice(h * HEAD_DIM, (h + 1) * HEAD_DIM)
            qT_ref[0, h, blk] = zqT[hd, rows].astype(BF16)
            vT_ref[0, h, blk] = jnp.concatenate([zvT[hd, rows], ones_row], axis=0).astype(BF16)
            pair = zk[rows, (h // 2) * LANES:(h // 2 + 1) * LANES]
            if h % 2:
                pair = pltpu.roll(pair, HEAD_DIM, 1)
            extra = pos_tab_ref[h] + blk_tab_ref[0, h, blk:blk + 1, :]
            kaug_ref[0, h, blk] = jnp.where(lane < HEAD_DIM, pair, extra).astype(BF16)


def _split_bf16(x, parts):
    out = []
    for _ in range(parts):
        piece = x.astype(jnp.bfloat16).astype(np.float64)
        out.append(piece.astype(np.float32))
        x = x - piece
    return out


def _alibi_tables(n_heads, nb, bpt):
    slopes = 2.0 ** (-8.0 * (np.arange(n_heads, dtype=np.float64) + 1.0) / n_heads) * LOG2E
    c0 = HEAD_DIM + SEL_ROWS
    pos_tab = np.zeros((n_heads, BLOCK, AUG), np.float32)
    for k, piece in enumerate(_split_bf16(slopes[:, None] * np.arange(BLOCK, dtype=np.float64)[None, :], BIAS_PARTS)):
        pos_tab[:, :, c0 + k] = piece
    blk_tab = np.zeros((nb, n_heads, AUG), np.float32)
    blk_tab[np.arange(nb), :, HEAD_DIM + np.arange(nb)] = 1.0
    blk_bias = slopes[None, :] * (BLOCK * np.arange(nb, dtype=np.float64))[:, None]
    for k, piece in enumerate(_split_bf16(blk_bias, BIAS_PARTS)):
        blk_tab[:, :, c0 + BIAS_PARTS + k] = piece
    blk_tab = blk_tab.reshape(nb // bpt, bpt, n_heads, AUG).transpose(0, 2, 1, 3)
    return jnp.asarray(pos_tab), jnp.asarray(blk_tab)


def _inproj_prompt(x1, w_in, b_in, cw, cb, cng, cnb, n_heads):
    bsz, t_len, d = x1.shape
    d_conv = cw.shape[1]
    d_att = n_heads * HEAD_DIM
    tm = 512 if t_len % 512 == 0 else BLOCK
    bpt = tm // BLOCK
    nb = t_len // BLOCK
    tiles = lambda w: pl.BlockSpec((1, tm, w), lambda b, t: (b, t, 0))
    per_seq = lambda r, w: pl.BlockSpec((1, r, w), lambda b, t: (b, 0, 0))
    head_blocks = lambda r, w: pl.BlockSpec((1, n_heads, bpt, r, w), lambda b, t: (b, 0, t, 0, 0))
    pos_tab, blk_tab = _alibi_tables(n_heads, nb, bpt)
    consts = (w_in, b_in, cw, cb, cng, cnb)
    return pl.pallas_call(
        functools.partial(_inproj_prompt_kernel, tm=tm, n_heads=n_heads, d_conv=d_conv, d_att=d_att),
        grid=(bsz, t_len // tm),
        in_specs=[tiles(d), _const_spec(pos_tab.shape),
                  pl.BlockSpec((1, n_heads, bpt, AUG), lambda b, t: (t, 0, 0, 0))]
                 + [_const_spec(c.shape) for c in consts],
        out_specs=[tiles(d_att), tiles(d_att), tiles(d_conv), per_seq(CONV_HALO, d_conv), per_seq(nb, d_att),
                   head_blocks(HEAD_DIM, BLOCK), head_blocks(BLOCK, AUG), head_blocks(V_ROWS, BLOCK)],
        out_shape=[jax.ShapeDtypeStruct((bsz, t_len, d_att), F32),
                   jax.ShapeDtypeStruct((bsz, t_len, d_att), F32),
                   jax.ShapeDtypeStruct((bsz, t_len, d_conv), BF16),
                   jax.ShapeDtypeStruct((bsz, CONV_HALO, d_conv), F32),
                   jax.ShapeDtypeStruct((bsz, nb, d_att), F32),
                   jax.ShapeDtypeStruct((bsz, n_heads, nb, HEAD_DIM, BLOCK), BF16),
                   jax.ShapeDtypeStruct((bsz, n_heads, nb, BLOCK, AUG), BF16),
                   jax.ShapeDtypeStruct((bsz, n_heads, nb, V_ROWS, BLOCK), BF16)],
        scratch_shapes=[pltpu.VMEM((CONV_HALO + tm, d_conv), F32)],
        compiler_params=_params(2),
        name="inproj_prompt",
    )(x1, pos_tab, blk_tab, *consts)


def _topk_select(gate, idx, n_valid):
    n = gate.shape[0]
    valid = idx < n_valid
    gm = jnp.where(valid, gate, -jnp.inf)
    rank = jnp.zeros(gate.shape, jnp.int32)
    for m in range(n):
        row = gm[m:m + 1, :]
        rank = rank + jnp.where(row > gm, 1, jnp.where(row == gm, jnp.where(idx > m, 1, 0), 0))
    return (rank < TOPK) & valid


def _moba_prompt_kernel(qT_ref, kaug_ref, vT_ref, kmean_ref, o_ref, qaug_scr, s_scr, m_scr, acc_scr,
                        *, nb, n_heads):
    i = pl.program_id(1)
    blk_idx = lax.broadcasted_iota(jnp.int32, (nb, BLOCK), 0)
    causal = (lax.broadcasted_iota(jnp.int32, (BLOCK, BLOCK), 0)
              <= lax.broadcasted_iota(jnp.int32, (BLOCK, BLOCK), 1))
    aug_row = lax.broadcasted_iota(jnp.int32, (AUG - HEAD_DIM - SEL_ROWS, BLOCK), 0)
    bias_rows = jnp.where(aug_row < 2 * BIAS_PARTS, 1.0, 0.0).astype(BF16)

    def absorb(h, st, j):
        m = m_scr[h]
        m_new = jnp.maximum(m, jnp.max(st, axis=0, keepdims=True))
        p = jnp.exp2(st - m_new).astype(BF16)
        acc_scr[h] = acc_scr[h] * jnp.exp2(m - m_new) + _dot(vT_ref[0, h, j], p)
        m_scr[h] = m_new

    gates = [_dot(kmean_ref[0, h].astype(BF16), qT_ref[0, h, 0]) for h in range(n_heads)]
    for h in range(n_heads):
        keep = _topk_select(gates[h], blk_idx, i) | (blk_idx == i)
        selbias = jnp.where(keep, 0.0, MASKED)
        if nb < SEL_ROWS:
            selbias = jnp.concatenate([selbias, jnp.zeros((SEL_ROWS - nb, BLOCK), F32)], axis=0)
        qaug_scr[h] = jnp.concatenate([qT_ref[0, h, 0], selbias.astype(BF16), bias_rows], axis=0)
        m_scr[h] = jnp.full((1, BLOCK), MASKED, F32)
        acc_scr[h] = jnp.zeros((V_ROWS, BLOCK), F32)
    for h in range(n_heads):
        s_scr[0, h] = _dot(kaug_ref[0, h, 0], qaug_scr[h])

    def past_block(j, cur):
        for h in range(n_heads):
            s_scr[1 - cur, h] = _dot(kaug_ref[0, h, j + 1], qaug_scr[h])
        for h in range(n_heads):
            absorb(h, s_scr[cur, h], j)

    def past_pair(t, carry):
        past_block(2 * t, 0)
        past_block(2 * t + 1, 1)
        return carry

    def own_block(cur):
        for h in range(n_heads):
            absorb(h, jnp.where(causal, s_scr[cur, h], MASKED), i)
        outs = [acc_scr[h, :HEAD_DIM, :] / acc_scr[h, HEAD_DIM:HEAD_DIM + 1, :] for h in range(n_heads)]
        o_ref[0] = jnp.concatenate(outs, axis=0).T.astype(BF16)

    lax.fori_loop(0, i // 2, past_pair, 0)

    @pl.when(i % 2 == 0)
    def _():
        own_block(0)

    @pl.when(i % 2 == 1)
    def _():
        past_block(i - 1, 0)
        own_block(1)


def _moba_prompt(qT, kaug, vT, kmean):
    bsz, n_heads, nb = qT.shape[:3]
    assert nb <= SEL_ROWS, "prompt sequences longer than SEL_ROWS key blocks are not supported"
    seq = lambda r, w: pl.BlockSpec((1, n_heads, nb, r, w), lambda b, i: (b, 0, 0, 0, 0))
    return pl.pallas_call(
        functools.partial(_moba_prompt_kernel, nb=nb, n_heads=n_heads),
        grid=(bsz, nb),
        in_specs=[pl.BlockSpec((1, n_heads, 1, HEAD_DIM, BLOCK), lambda b, i: (b, 0, i, 0, 0)),
                  seq(BLOCK, AUG), seq(V_ROWS, BLOCK),
                  pl.BlockSpec((1, n_heads, nb, HEAD_DIM), lambda b, i: (b, 0, 0, 0))],
        out_specs=pl.BlockSpec((1, BLOCK, n_heads * HEAD_DIM), lambda b, i: (b, i, 0)),
        out_shape=jax.ShapeDtypeStruct((bsz, nb * BLOCK, n_heads * HEAD_DIM), BF16),
        scratch_shapes=[pltpu.VMEM((n_heads, AUG, BLOCK), BF16),
                        pltpu.VMEM((2, n_heads, BLOCK, BLOCK), F32),
                        pltpu.VMEM((n_heads, 1, BLOCK), F32),
                        pltpu.VMEM((n_heads, V_ROWS, BLOCK), F32)],
        compiler_params=_params(2),
        name="moba_prompt",
    )(qT, kaug, vT, kmean)


def _inproj_sample_kernel(x1_ref, w_in_ref, b_in_ref, cw_ref, cb_ref, cng_ref, cnb_ref, state_ref,
                          q_ref, k_ref, v_ref, u_ref, conv_ref, *, bs, ts, d_conv, d_att):
    z = _dot(x1_ref[...].astype(BF16), w_in_ref[...]) + b_in_ref[...]
    a = z[:, :d_conv]
    g = z[:, d_conv:2 * d_conv]
    q_ref[...] = z[:, 2 * d_conv:2 * d_conv + d_att]
    k_ref[...] = z[:, 2 * d_conv + d_att:2 * d_conv + 2 * d_att]
    v_ref[...] = z[:, 2 * d_conv + 2 * d_att:]
    u_ref[...] = a * _sigmoid(g)
    n_state = CONV_WIDTH - 1
    for t in range(ts):
        y = jnp.zeros((bs, d_conv), F32)
        for j in range(CONV_WIDTH):
            tau = t + j
            if tau < n_state:
                slab = state_ref[tau]
            else:
                slab = u_ref[(tau - n_state) * bs:(tau - n_state + 1) * bs, :]
            y = y + cw_ref[j:j + 1, :] * slab
        conv_ref[t * bs:(t + 1) * bs, :] = _conv_branch_tail(y, cb_ref, cng_ref, cnb_ref).astype(BF16)


def _inproj_sample(x1, w_in, b_in, cw, cb, cng, cnb, state_tm, n_heads):
    n, d = x1.shape
    bs = state_tm.shape[1]
    ts = n // bs
    d_conv = cw.shape[1]
    d_att = n_heads * HEAD_DIM
    ins = (x1, w_in, b_in, cw, cb, cng, cnb, state_tm)
    out = lambda w, dt: jax.ShapeDtypeStruct((n, w), dt)
    outs = [out(d_att, F32), out(d_att, F32), out(d_att, F32), out(d_conv, F32), out(d_conv, BF16)]
    return pl.pallas_call(
        functools.partial(_inproj_sample_kernel, bs=bs, ts=ts, d_conv=d_conv, d_att=d_att),
        grid=(1,),
        in_specs=[_const_spec(a.shape) for a in ins],
        out_specs=[pl.BlockSpec(o.shape, lambda i: (0, 0)) for o in outs],
        out_shape=outs,
        compiler_params=_params(1),
        name="inproj_sample",
    )(*ins)


def _head_slopes(shape, n_heads, rows_per_head):
    head = lax.div(lax.broadcasted_iota(jnp.int32, shape, 0), jnp.int32(rows_per_head))
    out = jnp.zeros(shape, F32)
    for h in range(n_heads):
        out = jnp.where(head == h, 2.0 ** (-8.0 * (h + 1) / n_heads), out)
    return out


def _ffn_ln_moba_kernel(pt_ref, x_ref, wg_ref, wu_ref, wd_ref, g_ref, b_ref, q_ref, kn_ref, vn_ref, ck_hbm, cv_hbm,
                        o_ref, att_ref,
                        page_buf, sems, kmean_scr, s_scr, shift_scr, mx_scr, l_scr, acc_scr, qbd_scr, y_scr,
                        *, alpha, layer, n_heads, ts, n_pages):
    step = pl.program_id(0)
    seq = step // 2
    page = page_buf.shape[-1]
    d_att = n_heads * HEAD_DIM
    rows = n_heads * ts
    ppb = BLOCK // page
    nbs = n_pages // ppb
    n_chunks = n_pages // CHUNK_PAGES
    bpc = CHUNK_PAGES // ppb

    def page_copy(src_hbm, page_idx, slot):
        return pltpu.make_async_copy(src_hbm.at[layer, page_idx], page_buf.at[slot], sems.at[slot // CHUNK_PAGES])

    def start_chunk(src_hbm, seq_idx, ci):
        for slot in range(ci * CHUNK_PAGES, (ci + 1) * CHUNK_PAGES):
            page_copy(src_hbm, pt_ref[seq_idx * n_pages + slot], slot).start()

    def wait_chunk(src_hbm, ci):
        for slot in range(ci * CHUNK_PAGES, (ci + 1) * CHUNK_PAGES):
            page_copy(src_hbm, 0, slot).wait()

    def block_of(n):
        return jnp.concatenate([page_buf[n * ppb + i].reshape(d_att, page) for i in range(ppb)], axis=1)

    slope = _head_slopes((rows, 1), n_heads, ts)
    blk_lane = lax.broadcasted_iota(jnp.int32, (rows, nbs), 1)

    units = wg_ref.shape[1] // MXU_COLS
    assert units * MXU_COLS == wg_ref.shape[1]
    bounds = [(units * c // n_chunks) * MXU_COLS for c in range(n_chunks + 1)]

    def ffn_piece(ci):
        cols = slice(bounds[ci], bounds[ci + 1])
        xb = x_ref[...].astype(BF16)
        g = _dot(xb, wg_ref[:, cols])
        u = _dot(xb, wu_ref[:, cols])
        part = _dot(((g * _sigmoid(g)) * u).astype(BF16), wd_ref[cols, :])
        if ci == 0:
            y_scr[...] = part
        else:
            y_scr[...] += part

    def own_scores(qbd):
        pad = jnp.zeros((16 - ts, d_att), F32)
        kn = jnp.concatenate([kn_ref[0], pad], axis=0).astype(BF16)
        u_idx = lax.broadcasted_iota(jnp.int32, (rows, 16), 1)
        q_idx = lax.rem(lax.broadcasted_iota(jnp.int32, (rows, 16), 0), jnp.int32(ts))
        so = _dot_nt(qbd, kn) + slope * u_idx.astype(F32)
        return jnp.where(u_idx <= q_idx, so, MASKED)

    @pl.when(step % 2 == 0)
    def _():
        @pl.when(step == 0)
        def _():
            for ci in range(n_chunks):
                start_chunk(ck_hbm, seq, ci)

        qs = q_ref[0] * (HEAD_DIM ** -0.5)
        qrep = jnp.concatenate([qs] * n_heads, axis=0)
        row_head = lax.div(lax.broadcasted_iota(jnp.int32, (rows, d_att), 0), jnp.int32(ts))
        lane_head = lax.div(lax.broadcasted_iota(jnp.int32, (rows, d_att), 1), jnp.int32(HEAD_DIM))
        qbd = jnp.where(row_head == lane_head, qrep, 0.0).astype(BF16)
        qbd_scr[...] = qbd
        kmean_scr[...] = jnp.zeros(kmean_scr.shape, F32)
        key_bias = slope * lax.broadcasted_iota(jnp.int32, (rows, BLOCK), 1).astype(F32)
        km_lane = lax.broadcasted_iota(jnp.int32, kmean_scr.shape, 1)
        for ci in range(n_chunks):
            wait_chunk(ck_hbm, ci)
            ffn_piece(ci)
            kmeans = jnp.zeros(kmean_scr.shape, F32)
            for n in range(ci * bpc, (ci + 1) * bpc):
                kt = block_of(n)
                kmean = jnp.sum(kt, axis=1, keepdims=True) * (1.0 / BLOCK)
                kmeans = jnp.where(km_lane == n, kmean, kmeans)
                s_scr[n] = _dot(qbd, kt.astype(BF16)) + key_bias
            kmean_scr[...] += kmeans
            start_chunk(cv_hbm, seq, ci)

        gate = _dot(qbd, kmean_scr[...].astype(BF16))[:, :nbs]
        rank = jnp.zeros((rows, nbs), jnp.int32)
        for m in range(nbs):
            colm = gate[:, m:m + 1]
            rank = rank + jnp.where(colm > gate, 1, jnp.where(colm == gate, jnp.where(blk_lane > m, 1, 0), 0))
        sel = rank < TOPK
        blk_bias = slope * ((blk_lane - nbs) * BLOCK).astype(F32)
        blk_max = jnp.full((rows, nbs), MASKED, F32)
        for n in range(nbs):
            blk_max = jnp.where(blk_lane == n, jnp.max(s_scr[n], axis=1, keepdims=True), blk_max)
        e = jnp.where(sel, blk_max + blk_bias, MASKED)
        mx = jnp.maximum(jnp.max(e, axis=1, keepdims=True), jnp.max(own_scores(qbd), axis=1, keepdims=True))
        mx_scr[...] = mx
        shift_scr[...] = jnp.where(sel, blk_bias - mx, MASKED)
        l_scr[...] = jnp.zeros(l_scr.shape, F32)
        acc_scr[...] = jnp.zeros(acc_scr.shape, F32)

    @pl.when(step % 2 == 1)
    def _():
        shift = shift_scr[...]
        zero_rows = jnp.zeros((LANES - rows, BLOCK), F32)
        for ci in range(n_chunks):
            wait_chunk(cv_hbm, ci)
            ffn_piece(ci)
            blocks = range(ci * bpc, (ci + 1) * bpc)
            ps = [jnp.exp(s_scr[n] + shift[:, n:n + 1]) for n in blocks]
            l_scr[...] += sum(jnp.sum(p, axis=1, keepdims=True) for p in ps)
            pT = jnp.concatenate([jnp.concatenate([p, zero_rows], axis=0).T.astype(BF16) for p in ps], axis=0)
            vt = jnp.concatenate([block_of(n).astype(BF16) for n in blocks], axis=1)
            half = d_att // 2
            acc_scr[:half, :] += _dot(vt[:half], pT)
            acc_scr[half:, :] += _dot(vt[half:], pT)

            @pl.when(step + 1 < pl.num_programs(0))
            def _():
                start_chunk(ck_hbm, seq + 1, ci)

        po = jnp.exp(own_scores(qbd_scr[...]) - mx_scr[...])
        pad = jnp.zeros((16 - ts, d_att), F32)
        vn = jnp.concatenate([vn_ref[0], pad], axis=0).astype(BF16)
        num = acc_scr[...].T[:rows] + _dot(po.astype(BF16), vn)
        res = num / (l_scr[...] + jnp.sum(po, axis=1, keepdims=True))
        att_ref[0] = jnp.concatenate([res[h * ts:(h + 1) * ts, h * HEAD_DIM:(h + 1) * HEAD_DIM]
                                      for h in range(n_heads)], axis=-1)

    o_ref[...] = _layer_norm(alpha * x_ref[...] + 0.5 * y_scr[...], g_ref[...], b_ref[...])


def _ffn_ln_moba(x, wg, wu, wd, g, b, alpha, q, k_new, v_new, page_table, cache_k, cache_v, layer):
    n, d = x.shape
    bs, ts, d_att = q.shape
    n_pages = page_table.shape[1]
    n_heads, _, page = cache_k.shape[2:]
    steps = 2 * bs
    tm = n // steps
    assert tm * steps == n and tm % 8 == 0, "prompt rows must split evenly over two grid steps per sample sequence"
    assert BLOCK % page == 0 and (n_pages * page) % BLOCK == 0 and ts <= 8
    assert n_pages % CHUNK_PAGES == 0 and CHUNK_PAGES % (BLOCK // page) == 0
    nbs = n_pages * page // BLOCK
    rows = n_heads * ts
    assert nbs <= LANES and rows <= LANES
    pt = page_table.reshape(-1).astype(jnp.int32)
    row_tile = pl.BlockSpec((tm, d), lambda s, pt: (s, 0))
    seq = pl.BlockSpec((1, ts, d_att), lambda s, pt: (s // 2, 0, 0))
    hbm = pl.BlockSpec(memory_space=pl.ANY)
    consts = (wg, wu, wd, g, b)
    grid_spec = pltpu.PrefetchScalarGridSpec(
        num_scalar_prefetch=1,
        grid=(steps,),
        in_specs=[row_tile] + [_const_spec(c.shape) for c in consts] + [seq, seq, seq, hbm, hbm],
        out_specs=[row_tile, seq],
        scratch_shapes=[pltpu.VMEM((n_pages, n_heads, HEAD_DIM, page), F32),
                        pltpu.SemaphoreType.DMA((n_pages // CHUNK_PAGES,)),
                        pltpu.VMEM((d_att, LANES), F32),
                        pltpu.VMEM((nbs, rows, BLOCK), F32),
                        pltpu.VMEM((rows, nbs), F32),
                        pltpu.VMEM((rows, 1), F32),
                        pltpu.VMEM((rows, 1), F32),
                        pltpu.VMEM((d_att, LANES), F32),
                        pltpu.VMEM((rows, d_att), BF16),
                        pltpu.VMEM((tm, d), F32)],
    )
    return pl.pallas_call(
        functools.partial(_ffn_ln_moba_kernel, alpha=alpha, layer=layer, n_heads=n_heads, ts=ts, n_pages=n_pages),
        grid_spec=grid_spec,
        out_shape=[jax.ShapeDtypeStruct((n, d), F32), jax.ShapeDtypeStruct((bs, ts, d_att), F32)],
        compiler_params=_params(1),
        name="ffn_ln_moba",
    )(pt, x, *consts, q, k_new, v_new, cache_k, cache_v)


def kernel(x_prompt, x_sample, cache_k, cache_v, state_conv, page_table, p_prompt, p_sample,
           ffn1_w_gate, ffn1_w_up, ffn1_w_down, ln1_g, ln1_b, w_in, b_in, conv_w, conv_b,
           conv_norm_g, conv_norm_b, w_out, ln2_g, ln2_b, ffn2_w_gate, ffn2_w_up, ffn2_w_down,
           ln3_g, ln3_b, ple_w_proj, ple_w_gate):
    depth = ffn1_w_gate.shape[0]
    alpha = (2.0 * depth) ** 0.25
    bsz, t_len, d = x_prompt.shape
    bs, ts, _ = x_sample.shape
    n_heads = cache_k.shape[3]
    assert cache_k.shape[4] == HEAD_DIM and conv_w.shape[1] == CONV_WIDTH
    assert n_heads == 8, "ALiBi slopes must be powers of two for the exact bf16 bias columns"
    assert t_len % BLOCK == 0
    d_att = n_heads * HEAD_DIM
    d_conv = conv_w.shape[2]
    n_state = CONV_WIDTH - 1
    row = lambda a: a.reshape(1, -1)

    cache_kt = cache_k.transpose(0, 1, 3, 4, 2)
    cache_vt = cache_v.transpose(0, 1, 3, 4, 2)
    yp = x_prompt.reshape(bsz * t_len, d)
    ys = x_sample.transpose(1, 0, 2).reshape(ts * bs, d)
    outs = [[] for _ in range(6)]
    for i in range(depth):
        f1 = (ffn1_w_gate[i].astype(BF16), ffn1_w_up[i].astype(BF16), ffn1_w_down[i].astype(BF16),
              row(ln1_g[i]), row(ln1_b[i]))
        mix = (w_in[i].astype(BF16), row(b_in[i]), conv_w[i], row(conv_b[i]),
               row(conv_norm_g[i]), row(conv_norm_b[i]))
        tail = (w_out[i].astype(BF16), row(ln2_g[i]), row(ln2_b[i]),
                ffn2_w_gate[i].astype(BF16), ffn2_w_up[i].astype(BF16), ffn2_w_down[i].astype(BF16),
                row(ln3_g[i]), row(ln3_b[i]), ple_w_gate[i].astype(BF16), ple_w_proj[i].astype(BF16))

        x1s = _ffn_ln(ys, *f1, alpha)
        state = state_conv[i]
        q, ks, vs, u, convs = _inproj_sample(x1s, *mix, state.transpose(1, 0, 2), n_heads)
        to_bm = lambda a: a.reshape(ts, bs, -1).transpose(1, 0, 2)
        q, ks, vs, u = to_bm(q), to_bm(ks), to_bm(vs), to_bm(u)

        x1, atts = _ffn_ln_moba(yp, *f1, alpha, q, ks, vs, page_table, cache_kt, cache_vt, i)
        k, v, conv, utail, kmean, qT, kaug, vT = _inproj_prompt(x1.reshape(bsz, t_len, d), *mix, n_heads)
        nb = t_len // BLOCK
        kmean_h = kmean.reshape(bsz, nb, n_heads, HEAD_DIM).transpose(0, 2, 1, 3)
        att = _moba_prompt(qT, kaug, vT, kmean_h)
        yp = _out_ffn_ple(x1, conv.reshape(-1, d_conv), att.reshape(-1, d_att),
                          p_prompt[i].reshape(bsz * t_len, -1), *tail, alpha)
        outs[0].append(k.reshape(bsz, t_len, n_heads, HEAD_DIM))
        outs[1].append(v.reshape(bsz, t_len, n_heads, HEAD_DIM))
        outs[2].append(utail[:, CONV_HALO - n_state:])

        atts = atts.transpose(1, 0, 2).reshape(ts * bs, d_att).astype(BF16)
        ys = _out_ffn_ple(x1s, convs, atts, p_sample[i].transpose(1, 0, 2).reshape(ts * bs, -1), *tail, alpha)
        outs[3].append(ks.reshape(bs, ts, n_heads, HEAD_DIM))
        outs[4].append(vs.reshape(bs, ts, n_heads, HEAD_DIM))
        outs[5].append(jnp.concatenate([state, u], axis=1)[:, ts:])

    return (yp.reshape(bsz, t_len, d), ys.reshape(ts, bs, d).transpose(1, 0, 2),
            *(jnp.stack(o) for o in outs))
```

```python
import functools

import jax
import jax.numpy as jnp
import numpy as np
from jax import lax
from jax.experimental import pallas as pl
from jax.experimental.pallas import tpu as pltpu

F32 = jnp.float32
BF16 = jnp.bfloat16

LN_EPS = 1e-5
HEAD_DIM = 64
BLOCK = 256
TOPK = 3
CONV_WIDTH = 31
CONV_HALO = 32
CONV_ROWS = 32
SUBLANES = 8
MASKED = -(2.0 ** 100)
SEL_ROWS = 16
BIAS_PARTS = 4
AUG = 128
LOG2E = 1.4426950408889634
CHUNK_PAGES = 16
V_ROWS = 80
LANES = 128
MXU_COLS = 256
VMEM_LIMIT_BYTES = 56 * 1024 * 1024


def _dot(a, b):
    return jnp.dot(a, b, preferred_element_type=F32)


def _dot_nt(a, b):
    return lax.dot_general(a, b, (((1,), (1,)), ((), ())), preferred_element_type=F32)


def _sigmoid(x):
    return 1.0 / (1.0 + jnp.exp(-x))


def _layer_norm(r, g, b):
    mu = jnp.mean(r, axis=-1, keepdims=True)
    d = r - mu
    var = jnp.mean(d * d, axis=-1, keepdims=True)
    return d * lax.rsqrt(var + LN_EPS) * g + b


def _swiglu(xb, wg_ref, wu_ref, wd_ref, chunks=1):
    d_ff = wg_ref.shape[1]
    step = d_ff // chunks
    assert step * chunks == d_ff and step % LANES == 0
    y = None
    for c in range(chunks):
        cols = slice(c * step, (c + 1) * step)
        g = _dot(xb, wg_ref[:, cols])
        u = _dot(xb, wu_ref[:, cols])
        part = _dot(((g * _sigmoid(g)) * u).astype(BF16), wd_ref[cols, :])
        y = part if y is None else y + part
    return y


def _const_spec(shape):
    nd = len(shape)
    return pl.BlockSpec(shape, lambda *_: (0,) * nd, pipeline_mode=pl.Buffered(1))


def _params(n_axes):
    return pltpu.CompilerParams(dimension_semantics=("arbitrary",) * n_axes,
                                vmem_limit_bytes=VMEM_LIMIT_BYTES)


def _row_tile(n):
    for tm in (512, 256, 128, 64, 32, 16, 8):
        if n % tm == 0:
            return tm
    raise ValueError(f"row count {n} is not a multiple of 8")


def _ffn_ln_kernel(x_ref, wg_ref, wu_ref, wd_ref, g_ref, b_ref, o_ref, *, alpha):
    x = x_ref[...]
    y = _swiglu(x.astype(BF16), wg_ref, wu_ref, wd_ref)
    o_ref[...] = _layer_norm(alpha * x + 0.5 * y, g_ref[...], b_ref[...])


def _ffn_ln(x, wg, wu, wd, g, b, alpha):
    n, d = x.shape
    tm = _row_tile(n)
    return pl.pallas_call(
        functools.partial(_ffn_ln_kernel, alpha=alpha),
        grid=(n // tm,),
        in_specs=[pl.BlockSpec((tm, d), lambda i: (i, 0)),
                  _const_spec(wg.shape), _const_spec(wu.shape), _const_spec(wd.shape),
                  _const_spec(g.shape), _const_spec(b.shape)],
        out_specs=pl.BlockSpec((tm, d), lambda i: (i, 0)),
        out_shape=jax.ShapeDtypeStruct((n, d), F32),
        compiler_params=_params(1),
        name="ffn_ln",
    )(x, wg, wu, wd, g, b)


def _out_ffn_ple_kernel(x1_ref, conv_ref, att_ref, p_ref, wo_ref, g2_ref, b2_ref,
                        wg_ref, wu_ref, wd_ref, g3_ref, b3_ref, pg_ref, pp_ref, o_ref,
                        *, alpha, d_conv):
    m = _dot(conv_ref[...], wo_ref[:d_conv, :]) + _dot(att_ref[...], wo_ref[d_conv:, :])
    x2 = _layer_norm(alpha * x1_ref[...] + m, g2_ref[...], b2_ref[...])
    y = _swiglu(x2.astype(BF16), wg_ref, wu_ref, wd_ref)
    x3 = _layer_norm(alpha * x2 + 0.5 * y, g3_ref[...], b3_ref[...])
    gate = _sigmoid(_dot(x3.astype(BF16), pg_ref[...]))
    o_ref[...] = x3 + gate * _dot(p_ref[...].astype(BF16), pp_ref[...])


def _out_ffn_ple(x1, conv, att, p, wo, g2, b2, wg, wu, wd, g3, b3, pg, pp, alpha):
    n, d = x1.shape
    tm = _row_tile(n)
    rows = lambda a: pl.BlockSpec((tm, a.shape[1]), lambda i: (i, 0))
    consts = (wo, g2, b2, wg, wu, wd, g3, b3, pg, pp)
    return pl.pallas_call(
        functools.partial(_out_ffn_ple_kernel, alpha=alpha, d_conv=conv.shape[1]),
        grid=(n // tm,),
        in_specs=[rows(x1), rows(conv), rows(att), rows(p)] + [_const_spec(c.shape) for c in consts],
        out_specs=rows(x1),
        out_shape=jax.ShapeDtypeStruct((n, d), F32),
        compiler_params=_params(1),
        name="out_ffn_ple",
    )(x1, conv, att, p, *consts)


def _conv_branch_tail(y, cb_ref, cng_ref, cnb_ref):
    c = _layer_norm(y + cb_ref[...], cng_ref[...], cnb_ref[...])
    return c * _sigmoid(c)


def _inproj_prompt_kernel(x1_ref, pos_tab_ref, blk_tab_ref, w_in_ref, b_in_ref, cw_ref, cb_ref, cng_ref, cnb_ref,
                          k_ref, v_ref, conv_ref, utail_ref, kmean_ref, qT_ref, kaug_ref, vT_ref,
                          uext_ref, *, tm, n_heads, d_conv, d_att):
    t = pl.program_id(1)
    bpt = tm // BLOCK
    z = _dot(x1_ref[0].astype(BF16), w_in_ref[...]) + b_in_ref[...]
    a = z[:, :d_conv]
    g = z[:, d_conv:2 * d_conv]
    zq = z[:, 2 * d_conv:2 * d_conv + d_att]
    zk = z[:, 2 * d_conv + d_att:2 * d_conv + 2 * d_att]
    zv = z[:, 2 * d_conv + 2 * d_att:]
    k_ref[0] = zk
    v_ref[0] = zv

    @pl.when(t == 0)
    def _():
        uext_ref[0:CONV_HALO, :] = jnp.zeros((CONV_HALO, d_conv), F32)

    uext_ref[CONV_HALO:CONV_HALO + tm, :] = a * _sigmoid(g)
    off = CONV_HALO - (CONV_WIDTH - 1)
    for r0 in range(0, tm, CONV_ROWS):
        window = uext_ref[r0:r0 + CONV_ROWS + CONV_HALO, :]
        n_win = CONV_ROWS + CONV_HALO
        shifted = [window] + [pltpu.roll(window, n_win - s, 0) for s in range(1, SUBLANES)]
        y = jnp.zeros((CONV_ROWS, d_conv), F32)
        for j in range(CONV_WIDTH):
            s = (off + j) % SUBLANES
            base = off + j - s
            y = y + cw_ref[j:j + 1, :] * shifted[s][base:base + CONV_ROWS, :]
        conv_ref[0, r0:r0 + CONV_ROWS, :] = _conv_branch_tail(y, cb_ref, cng_ref, cnb_ref).astype(BF16)
    tail = uext_ref[tm:tm + CONV_HALO, :]
    utail_ref[0] = tail
    uext_ref[0:CONV_HALO, :] = tail

    @pl.when(t == 0)
    def _():
        kmean_ref[0] = jnp.zeros(kmean_ref.shape[1:], F32)

    km = kmean_ref[0]
    km_row = lax.broadcasted_iota(jnp.int32, km.shape, 0)
    for blk in range(bpt):
        mean = jnp.sum(zk[blk * BLOCK:(blk + 1) * BLOCK], axis=0, keepdims=True) * (1.0 / BLOCK)
        km = jnp.where(km_row == t * bpt + blk, mean, km)
    kmean_ref[0] = km

    zqT = (zq * (HEAD_DIM ** -0.5 * LOG2E)).T
    zvT = zv.T
    lane = lax.broadcasted_iota(jnp.int32, (BLOCK, LANES), 1)
    ones_row = jnp.where(lax.broadcasted_iota(jnp.int32, (V_ROWS - HEAD_DIM, BLOCK), 0) == 0, 1.0, 0.0)
    for blk in range(bpt):
        rows = slice(blk * BLOCK, (blk + 1) * BLOCK)
        for h in range(n_heads):
            hd = slice(h * HEAD_DIM, (h + 1) * HEAD_DIM)
            qT_ref[0, h, blk] = zqT[hd, rows].astype(BF16)
            vT_ref[0, h, blk] = jnp.concatenate([zvT[hd, rows], ones_row], axis=0).astype(BF16)
            pair = zk[rows, (h // 2) * LANES:(h // 2 + 1) * LANES]
            if h % 2:
                pair = pltpu.roll(pair, HEAD_DIM, 1)
            extra = pos_tab_ref[h] + blk_tab_ref[0, h, blk:blk + 1, :]
            kaug_ref[0, h, blk] = jnp.where(lane < HEAD_DIM, pair, extra).astype(BF16)


def _split_bf16(x, parts):
    out = []
    for _ in range(parts):
        piece = x.astype(jnp.bfloat16).astype(np.float64)
        out.append(piece.astype(np.float32))
        x = x - piece
    return out


def _alibi_tables(n_heads, nb, bpt):
    slopes = 2.0 ** (-8.0 * (np.arange(n_heads, dtype=np.float64) + 1.0) / n_heads) * LOG2E
    c0 = HEAD_DIM + SEL_ROWS
    pos_tab = np.zeros((n_heads, BLOCK, AUG), np.float32)
    for k, piece in enumerate(_split_bf16(slopes[:, None] * np.arange(BLOCK, dtype=np.float64)[None, :], BIAS_PARTS)):
        pos_tab[:, :, c0 + k] = piece
    blk_tab = np.zeros((nb, n_heads, AUG), np.float32)
    blk_tab[np.arange(nb), :, HEAD_DIM + np.arange(nb)] = 1.0
    blk_bias = slopes[None, :] * (BLOCK * np.arange(nb, dtype=np.float64))[:, None]
    for k, piece in enumerate(_split_bf16(blk_bias, BIAS_PARTS)):
        blk_tab[:, :, c0 + BIAS_PARTS + k] = piece
    blk_tab = blk_tab.reshape(nb // bpt, bpt, n_heads, AUG).transpose(0, 2, 1, 3)
    return jnp.asarray(pos_tab), jnp.asarray(blk_tab)


def _inproj_prompt(x1, w_in, b_in, cw, cb, cng, cnb, n_heads):
    bsz, t_len, d = x1.shape
    d_conv = cw.shape[1]
    d_att = n_heads * HEAD_DIM
    tm = 512 if t_len % 512 == 0 else BLOCK
    bpt = tm // BLOCK
    nb = t_len // BLOCK
    tiles = lambda w: pl.BlockSpec((1, tm, w), lambda b, t: (b, t, 0))
    per_seq = lambda r, w: pl.BlockSpec((1, r, w), lambda b, t: (b, 0, 0))
    head_blocks = lambda r, w: pl.BlockSpec((1, n_heads, bpt, r, w), lambda b, t: (b, 0, t, 0, 0))
    pos_tab, blk_tab = _alibi_tables(n_heads, nb, bpt)
    consts = (w_in, b_in, cw, cb, cng, cnb)
    return pl.pallas_call(
        functools.partial(_inproj_prompt_kernel, tm=tm, n_heads=n_heads, d_conv=d_conv, d_att=d_att),
        grid=(bsz, t_len // tm),
        in_specs=[tiles(d), _const_spec(pos_tab.shape),
                  pl.BlockSpec((1, n_heads, bpt, AUG), lambda b, t: (t, 0, 0, 0))]
                 + [_const_spec(c.shape) for c in consts],
        out_specs=[tiles(d_att), tiles(d_att), tiles(d_conv), per_seq(CONV_HALO, d_conv), per_seq(nb, d_att),
                   head_blocks(HEAD_DIM, BLOCK), head_blocks(BLOCK, AUG), head_blocks(V_ROWS, BLOCK)],
        out_shape=[jax.ShapeDtypeStruct((bsz, t_len, d_att), F32),
                   jax.ShapeDtypeStruct((bsz, t_len, d_att), F32),
                   jax.ShapeDtypeStruct((bsz, t_len, d_conv), BF16),
                   jax.ShapeDtypeStruct((bsz, CONV_HALO, d_conv), F32),
                   jax.ShapeDtypeStruct((bsz, nb, d_att), F32),
                   jax.ShapeDtypeStruct((bsz, n_heads, nb, HEAD_DIM, BLOCK), BF16),
                   jax.ShapeDtypeStruct((bsz, n_heads, nb, BLOCK, AUG), BF16),
                   jax.ShapeDtypeStruct((bsz, n_heads, nb, V_ROWS, BLOCK), BF16)],
        scratch_shapes=[pltpu.VMEM((CONV_HALO + tm, d_conv), F32)],
        compiler_params=_params(2),
        name="inproj_prompt",
    )(x1, pos_tab, blk_tab, *consts)


def _topk_select(gate, idx, n_valid):
    n = gate.shape[0]
    valid = idx < n_valid
    gm = jnp.where(valid, gate, -jnp.inf)
    rank = jnp.zeros(gate.shape, jnp.int32)
    for m in range(n):
        row = gm[m:m + 1, :]
        rank = rank + jnp.where(row > gm, 1, jnp.where(row == gm, jnp.where(idx > m, 1, 0), 0))
    return (rank < TOPK) & valid


def _moba_prompt_kernel(qT_ref, kaug_ref, vT_ref, kmean_ref, o_ref, qaug_scr, s_scr, m_scr, acc_scr,
                        *, nb, n_heads):
    i = pl.program_id(1)
    blk_idx = lax.broadcasted_iota(jnp.int32, (nb, BLOCK), 0)
    causal = (lax.broadcasted_iota(jnp.int32, (BLOCK, BLOCK), 0)
              <= lax.broadcasted_iota(jnp.int32, (BLOCK, BLOCK), 1))
    aug_row = lax.broadcasted_iota(jnp.int32, (AUG - HEAD_DIM - SEL_ROWS, BLOCK), 0)
    bias_rows = jnp.where(aug_row < 2 * BIAS_PARTS, 1.0, 0.0).astype(BF16)

    def absorb(h, st, j):
        m = m_scr[h]
        m_new = jnp.maximum(m, jnp.max(st, axis=0, keepdims=True))
        p = jnp.exp2(st - m_new).astype(BF16)
        acc_scr[h] = acc_scr[h] * jnp.exp2(m - m_new) + _dot(vT_ref[0, h, j], p)
        m_scr[h] = m_new

    gates = [_dot(kmean_ref[0, h].astype(BF16), qT_ref[0, h, 0]) for h in range(n_heads)]
    for h in range(n_heads):
        keep = _topk_select(gates[h], blk_idx, i) | (blk_idx == i)
        selbias = jnp.where(keep, 0.0, MASKED)
        if nb < SEL_ROWS:
            selbias = jnp.concatenate([selbias, jnp.zeros((SEL_ROWS - nb, BLOCK), F32)], axis=0)
        qaug_scr[h] = jnp.concatenate([qT_ref[0, h, 0], selbias.astype(BF16), bias_rows], axis=0)
        m_scr[h] = jnp.full((1, BLOCK), MASKED, F32)
        acc_scr[h] = jnp.zeros((V_ROWS, BLOCK), F32)
    for h in range(n_heads):
        s_scr[0, h] = _dot(kaug_ref[0, h, 0], qaug_scr[h])

    def past_block(j, cur):
        for h in range(n_heads):
            s_scr[1 - cur, h] = _dot(kaug_ref[0, h, j + 1], qaug_scr[h])
        for h in range(n_heads):
            absorb(h, s_scr[cur, h], j)

    def past_pair(t, carry):
        past_block(2 * t, 0)
        past_block(2 * t + 1, 1)
        return carry

    def own_block(cur):
        for h in range(n_heads):
            absorb(h, jnp.where(causal, s_scr[cur, h], MASKED), i)
        outs = [acc_scr[h, :HEAD_DIM, :] / acc_scr[h, HEAD_DIM:HEAD_DIM + 1, :] for h in range(n_heads)]
        o_ref[0] = jnp.concatenate(outs, axis=0).T.astype(BF16)

    lax.fori_loop(0, i // 2, past_pair, 0)

    @pl.when(i % 2 == 0)
    def _():
        own_block(0)

    @pl.when(i % 2 == 1)
    def _():
        past_block(i - 1, 0)
        own_block(1)


def _moba_prompt(qT, kaug, vT, kmean):
    bsz, n_heads, nb = qT.shape[:3]
    assert nb <= SEL_ROWS, "prompt sequences longer than SEL_ROWS key blocks are not supported"
    seq = lambda r, w: pl.BlockSpec((1, n_heads, nb, r, w), lambda b, i: (b, 0, 0, 0, 0))
    return pl.pallas_call(
        functools.partial(_moba_prompt_kernel, nb=nb, n_heads=n_heads),
        grid=(bsz, nb),
        in_specs=[pl.BlockSpec((1, n_heads, 1, HEAD_DIM, BLOCK), lambda b, i: (b, 0, i, 0, 0)),
                  seq(BLOCK, AUG), seq(V_ROWS, BLOCK),
                  pl.BlockSpec((1, n_heads, nb, HEAD_DIM), lambda b, i: (b, 0, 0, 0))],
        out_specs=pl.BlockSpec((1, BLOCK, n_heads * HEAD_DIM), lambda b, i: (b, i, 0)),
        out_shape=jax.ShapeDtypeStruct((bsz, nb * BLOCK, n_heads * HEAD_DIM), BF16),
        scratch_shapes=[pltpu.VMEM((n_heads, AUG, BLOCK), BF16),
                        pltpu.VMEM((2, n_heads, BLOCK, BLOCK), F32),
                        pltpu.VMEM((n_heads, 1, BLOCK), F32),
                        pltpu.VMEM((n_heads, V_ROWS, BLOCK), F32)],
        compiler_params=_params(2),
        name="moba_prompt",
    )(qT, kaug, vT, kmean)


def _inproj_sample_kernel(x1_ref, w_in_ref, b_in_ref, cw_ref, cb_ref, cng_ref, cnb_ref, state_ref,
                          q_ref, k_ref, v_ref, u_ref, conv_ref, *, bs, ts, d_conv, d_att):
    z = _dot(x1_ref[...].astype(BF16), w_in_ref[...]) + b_in_ref[...]
    a = z[:, :d_conv]
    g = z[:, d_conv:2 * d_conv]
    q_ref[...] = z[:, 2 * d_conv:2 * d_conv + d_att]
    k_ref[...] = z[:, 2 * d_conv + d_att:2 * d_conv + 2 * d_att]
    v_ref[...] = z[:, 2 * d_conv + 2 * d_att:]
    u_ref[...] = a * _sigmoid(g)
    n_state = CONV_WIDTH - 1
    for t in range(ts):
        y = jnp.zeros((bs, d_conv), F32)
        for j in range(CONV_WIDTH):
            tau = t + j
            if tau < n_state:
                slab = state_ref[tau]
            else:
                slab = u_ref[(tau - n_state) * bs:(tau - n_state + 1) * bs, :]
            y = y + cw_ref[j:j + 1, :] * slab
        conv_ref[t * bs:(t + 1) * bs, :] = _conv_branch_tail(y, cb_ref, cng_ref, cnb_ref).astype(BF16)


def _inproj_sample(x1, w_in, b_in, cw, cb, cng, cnb, state_tm, n_heads):
    n, d = x1.shape
    bs = state_tm.shape[1]
    ts = n // bs
    d_conv = cw.shape[1]
    d_att = n_heads * HEAD_DIM
    ins = (x1, w_in, b_in, cw, cb, cng, cnb, state_tm)
    out = lambda w, dt: jax.ShapeDtypeStruct((n, w), dt)
    outs = [out(d_att, F32), out(d_att, F32), out(d_att, F32), out(d_conv, F32), out(d_conv, BF16)]
    return pl.pallas_call(
        functools.partial(_inproj_sample_kernel, bs=bs, ts=ts, d_conv=d_conv, d_att=d_att),
        grid=(1,),
        in_specs=[_const_spec(a.shape) for a in ins],
        out_specs=[pl.BlockSpec(o.shape, lambda i: (0, 0)) for o in outs],
        out_shape=outs,
        compiler_params=_params(1),
        name="inproj_sample",
    )(*ins)


def _head_slopes(shape, n_heads, rows_per_head):
    head = lax.div(lax.broadcasted_iota(jnp.int32, shape, 0), jnp.int32(rows_per_head))
    out = jnp.zeros(shape, F32)
    for h in range(n_heads):
        out = jnp.where(head == h, 2.0 ** (-8.0 * (h + 1) / n_heads), out)
    return out


def _ffn_ln_moba_kernel(pt_ref, x_ref, wg_ref, wu_ref, wd_ref, g_ref, b_ref, q_ref, kn_ref, vn_ref, ck_hbm, cv_hbm,
                        o_ref, att_ref,
                        page_buf, sems, kmean_scr, s_scr, shift_scr, mx_scr, l_scr, acc_scr, qbd_scr, y_scr,
                        *, alpha, layer, n_heads, ts, n_pages):
    step = pl.program_id(0)
    seq = step // 2
    page = page_buf.shape[-1]
    d_att = n_heads * HEAD_DIM
    rows = n_heads * ts
    ppb = BLOCK // page
    nbs = n_pages // ppb
    n_chunks = n_pages // CHUNK_PAGES
    bpc = CHUNK_PAGES // ppb

    def page_copy(src_hbm, page_idx, slot):
        return pltpu.make_async_copy(src_hbm.at[layer, page_idx], page_buf.at[slot], sems.at[slot // CHUNK_PAGES])

    def start_chunk(src_hbm, seq_idx, ci):
        for slot in range(ci * CHUNK_PAGES, (ci + 1) * CHUNK_PAGES):
            page_copy(src_hbm, pt_ref[seq_idx * n_pages + slot], slot).start(priority=slot % 2)

    def wait_chunk(src_hbm, ci):
        for slot in range(ci * CHUNK_PAGES, (ci + 1) * CHUNK_PAGES):
            page_copy(src_hbm, 0, slot).wait()

    def block_of(n):
        return jnp.concatenate([page_buf[n * ppb + i].reshape(d_att, page) for i in range(ppb)], axis=1)

    slope = _head_slopes((rows, 1), n_heads, ts)
    blk_lane = lax.broadcasted_iota(jnp.int32, (rows, nbs), 1)

    units = wg_ref.shape[1] // MXU_COLS
    assert units * MXU_COLS == wg_ref.shape[1]
    bounds = [(units * c // n_chunks) * MXU_COLS for c in range(n_chunks + 1)]

    def ffn_piece(ci):
        cols = slice(bounds[ci], bounds[ci + 1])
        xb = x_ref[...].astype(BF16)
        g = _dot(xb, wg_ref[:, cols])
        u = _dot(xb, wu_ref[:, cols])
        part = _dot(((g * _sigmoid(g)) * u).astype(BF16), wd_ref[cols, :])
        if ci == 0:
            y_scr[...] = part
        else:
            y_scr[...] += part

    def own_scores(qbd):
        pad = jnp.zeros((16 - ts, d_att), F32)
        kn = jnp.concatenate([kn_ref[0], pad], axis=0).astype(BF16)
        u_idx = lax.broadcasted_iota(jnp.int32, (rows, 16), 1)
        q_idx = lax.rem(lax.broadcasted_iota(jnp.int32, (rows, 16), 0), jnp.int32(ts))
        so = _dot_nt(qbd, kn) + slope * u_idx.astype(F32)
        return jnp.where(u_idx <= q_idx, so, MASKED)

    @pl.when(step % 2 == 0)
    def _():
        @pl.when(step == 0)
        def _():
            for ci in range(n_chunks):
                start_chunk(ck_hbm, seq, ci)

        qs = q_ref[0] * (HEAD_DIM ** -0.5)
        qrep = jnp.concatenate([qs] * n_heads, axis=0)
        row_head = lax.div(lax.broadcasted_iota(jnp.int32, (rows, d_att), 0), jnp.int32(ts))
        lane_head = lax.div(lax.broadcasted_iota(jnp.int32, (rows, d_att), 1), jnp.int32(HEAD_DIM))
        qbd = jnp.where(row_head == lane_head, qrep, 0.0).astype(BF16)
        qbd_scr[...] = qbd
        kmean_scr[...] = jnp.zeros(kmean_scr.shape, F32)
        key_bias = slope * lax.broadcasted_iota(jnp.int32, (rows, BLOCK), 1).astype(F32)
        km_lane = lax.broadcasted_iota(jnp.int32, kmean_scr.shape, 1)
        for ci in range(n_chunks):
            wait_chunk(ck_hbm, ci)
            ffn_piece(ci)
            kmeans = jnp.zeros(kmean_scr.shape, F32)
            for n in range(ci * bpc, (ci + 1) * bpc):
                kt = block_of(n)
                kmean = jnp.sum(kt, axis=1, keepdims=True) * (1.0 / BLOCK)
                kmeans = jnp.where(km_lane == n, kmean, kmeans)
                s_scr[n] = _dot(qbd, kt.astype(BF16)) + key_bias
            kmean_scr[...] += kmeans
            start_chunk(cv_hbm, seq, ci)

        gate = _dot(qbd, kmean_scr[...].astype(BF16))[:, :nbs]
        rank = jnp.zeros((rows, nbs), jnp.int32)
        for m in range(nbs):
            colm = gate[:, m:m + 1]
            rank = rank + jnp.where(colm > gate, 1, jnp.where(colm == gate, jnp.where(blk_lane > m, 1, 0), 0))
        sel = rank < TOPK
        blk_bias = slope * ((blk_lane - nbs) * BLOCK).astype(F32)
        blk_max = jnp.full((rows, nbs), MASKED, F32)
        for n in range(nbs):
            blk_max = jnp.where(blk_lane == n, jnp.max(s_scr[n], axis=1, keepdims=True), blk_max)
        e = jnp.where(sel, blk_max + blk_bias, MASKED)
        mx = jnp.maximum(jnp.max(e, axis=1, keepdims=True), jnp.max(own_scores(qbd), axis=1, keepdims=True))
        mx_scr[...] = mx
        shift_scr[...] = jnp.where(sel, blk_bias - mx, MASKED)
        l_scr[...] = jnp.zeros(l_scr.shape, F32)
        acc_scr[...] = jnp.zeros(acc_scr.shape, F32)

    @pl.when(step % 2 == 1)
    def _():
        shift = shift_scr[...]
        zero_rows = jnp.zeros((LANES - rows, BLOCK), F32)
        for ci in range(n_chunks):
            wait_chunk(cv_hbm, ci)
            ffn_piece(ci)
            blocks = range(ci * bpc, (ci + 1) * bpc)
            ps = [jnp.exp(s_scr[n] + shift[:, n:n + 1]) for n in blocks]
            l_scr[...] += sum(jnp.sum(p, axis=1, keepdims=True) for p in ps)
            pT = jnp.concatenate([jnp.concatenate([p, zero_rows], axis=0).T.astype(BF16) for p in ps], axis=0)
            vt = jnp.concatenate([block_of(n).astype(BF16) for n in blocks], axis=1)
            half = d_att // 2
            acc_scr[:half, :] += _dot(vt[:half], pT)
            acc_scr[half:, :] += _dot(vt[half:], pT)

            @pl.when(step + 1 < pl.num_programs(0))
            def _():
                start_chunk(ck_hbm, seq + 1, ci)

        po = jnp.exp(own_scores(qbd_scr[...]) - mx_scr[...])
        pad = jnp.zeros((16 - ts, d_att), F32)
        vn = jnp.concatenate([vn_ref[0], pad], axis=0).astype(BF16)
        num = acc_scr[...].T[:rows] + _dot(po.astype(BF16), vn)
        res = num / (l_scr[...] + jnp.sum(po, axis=1, keepdims=True))
        att_ref[0] = jnp.concatenate([res[h * ts:(h + 1) * ts, h * HEAD_DIM:(h + 1) * HEAD_DIM]
                                      for h in range(n_heads)], axis=-1)

    o_ref[...] = _layer_norm(alpha * x_ref[...] + 0.5 * y_scr[...], g_ref[...], b_ref[...])


def _ffn_ln_moba(x, wg, wu, wd, g, b, alpha, q, k_new, v_new, page_table, cache_k, cache_v, layer):
    n, d = x.shape
    bs, ts, d_att = q.shape
    n_pages = page_table.shape[1]
    n_heads, _, page = cache_k.shape[2:]
    steps = 2 * bs
    tm = n // steps
    assert tm * steps == n and tm % 8 == 0, "prompt rows must split evenly over two grid steps per sample sequence"
    assert BLOCK % page == 0 and (n_pages * page) % BLOCK == 0 and ts <= 8
    assert n_pages % CHUNK_PAGES == 0 and CHUNK_PAGES % (BLOCK // page) == 0
    nbs = n_pages * page // BLOCK
    rows = n_heads * ts
    assert nbs <= LANES and rows <= LANES
    pt = page_table.reshape(-1).astype(jnp.int32)
    row_tile = pl.BlockSpec((tm, d), lambda s, pt: (s, 0))
    seq = pl.BlockSpec((1, ts, d_att), lambda s, pt: (s // 2, 0, 0))
    hbm = pl.BlockSpec(memory_space=pl.ANY)
    consts = (wg, wu, wd, g, b)
    grid_spec = pltpu.PrefetchScalarGridSpec(
        num_scalar_prefetch=1,
        grid=(steps,),
        in_specs=[row_tile] + [_const_spec(c.shape) for c in consts] + [seq, seq, seq, hbm, hbm],
        out_specs=[row_tile, seq],
        scratch_shapes=[pltpu.VMEM((n_pages, n_heads, HEAD_DIM, page), F32),
                        pltpu.SemaphoreType.DMA((n_pages // CHUNK_PAGES,)),
                        pltpu.VMEM((d_att, LANES), F32),
                        pltpu.VMEM((nbs, rows, BLOCK), F32),
                        pltpu.VMEM((rows, nbs), F32),
                        pltpu.VMEM((rows, 1), F32),
                        pltpu.VMEM((rows, 1), F32),
                        pltpu.VMEM((d_att, LANES), F32),
                        pltpu.VMEM((rows, d_att), BF16),
                        pltpu.VMEM((tm, d), F32)],
    )
    return pl.pallas_call(
        functools.partial(_ffn_ln_moba_kernel, alpha=alpha, layer=layer, n_heads=n_heads, ts=ts, n_pages=n_pages),
        grid_spec=grid_spec,
        out_shape=[jax.ShapeDtypeStruct((n, d), F32), jax.ShapeDtypeStruct((bs, ts, d_att), F32)],
        compiler_params=_params(1),
        name="ffn_ln_moba",
    )(pt, x, *consts, q, k_new, v_new, cache_k, cache_v)


def kernel(x_prompt, x_sample, cache_k, cache_v, state_conv, page_table, p_prompt, p_sample,
           ffn1_w_gate, ffn1_w_up, ffn1_w_down, ln1_g, ln1_b, w_in, b_in, conv_w, conv_b,
           conv_norm_g, conv_norm_b, w_out, ln2_g, ln2_b, ffn2_w_gate, ffn2_w_up, ffn2_w_down,
           ln3_g, ln3_b, ple_w_proj, ple_w_gate):
    depth = ffn1_w_gate.shape[0]
    alpha = (2.0 * depth) ** 0.25
    bsz, t_len, d = x_prompt.shape
    bs, ts, _ = x_sample.shape
    n_heads = cache_k.shape[3]
    assert cache_k.shape[4] == HEAD_DIM and conv_w.shape[1] == CONV_WIDTH
    assert n_heads == 8, "ALiBi slopes must be powers of two for the exact bf16 bias columns"
    assert t_len % BLOCK == 0
    d_att = n_heads * HEAD_DIM
    d_conv = conv_w.shape[2]
    n_state = CONV_WIDTH - 1
    row = lambda a: a.reshape(1, -1)

    cache_kt = cache_k.transpose(0, 1, 3, 4, 2)
    cache_vt = cache_v.transpose(0, 1, 3, 4, 2)
    yp = x_prompt.reshape(bsz * t_len, d)
    ys = x_sample.transpose(1, 0, 2).reshape(ts * bs, d)
    outs = [[] for _ in range(6)]
    for i in range(depth):
        f1 = (ffn1_w_gate[i].astype(BF16), ffn1_w_up[i].astype(BF16), ffn1_w_down[i].astype(BF16),
              row(ln1_g[i]), row(ln1_b[i]))
        mix = (w_in[i].astype(BF16), row(b_in[i]), conv_w[i], row(conv_b[i]),
               row(conv_norm_g[i]), row(conv_norm_b[i]))
        tail = (w_out[i].astype(BF16), row(ln2_g[i]), row(ln2_b[i]),
                ffn2_w_gate[i].astype(BF16), ffn2_w_up[i].astype(BF16), ffn2_w_down[i].astype(BF16),
                row(ln3_g[i]), row(ln3_b[i]), ple_w_gate[i].astype(BF16), ple_w_proj[i].astype(BF16))

        x1s = _ffn_ln(ys, *f1, alpha)
        state = state_conv[i]
        q, ks, vs, u, convs = _inproj_sample(x1s, *mix, state.transpose(1, 0, 2), n_heads)
        to_bm = lambda a: a.reshape(ts, bs, -1).transpose(1, 0, 2)
        q, ks, vs, u = to_bm(q), to_bm(ks), to_bm(vs), to_bm(u)

        x1, atts = _ffn_ln_moba(yp, *f1, alpha, q, ks, vs, page_table, cache_kt, cache_vt, i)
        k, v, conv, utail, kmean, qT, kaug, vT = _inproj_prompt(x1.reshape(bsz, t_len, d), *mix, n_heads)
        nb = t_len // BLOCK
        kmean_h = kmean.reshape(bsz, nb, n_heads, HEAD_DIM).transpose(0, 2, 1, 3)
        att = _moba_prompt(qT, kaug, vT, kmean_h)
        yp = _out_ffn_ple(x1, conv.reshape(-1, d_conv), att.reshape(-1, d_att),
                          p_prompt[i].reshape(bsz * t_len, -1), *tail, alpha)
        outs[0].append(k.reshape(bsz, t_len, n_heads, HEAD_DIM))
        outs[1].append(v.reshape(bsz, t_len, n_heads, HEAD_DIM))
        outs[2].append(utail[:, CONV_HALO - n_state:])

        atts = atts.transpose(1, 0, 2).reshape(ts * bs, d_att).astype(BF16)
        ys = _out_ffn_ple(x1s, convs, atts, p_sample[i].transpose(1, 0, 2).reshape(ts * bs, -1), *tail, alpha)
        outs[3].append(ks.reshape(bs, ts, n_heads, HEAD_DIM))
        outs[4].append(vs.reshape(bs, ts, n_heads, HEAD_DIM))
        outs[5].append(jnp.concatenate([state, u], axis=1)[:, ts:])

    return (yp.reshape(bsz, t_len, d), ys.reshape(ts, bs, d).transpose(1, 0, 2),
            *(jnp.stack(o) for o in outs))
```
